```python
import jax, jax.numpy as jnp
from jax import lax
import numpy as np

D_MODEL = 1024
BATCH = 2
SEQ = 16384
DEPTH = 2

N_MIXERS = 2
N_A_LAYERS = (DEPTH + 1) // 2
N_B_LAYERS = DEPTH // 2
RMS_EPS = 1e-6

RWKV_HEAD = 64
RWKV_HEADS = D_MODEL // RWKV_HEAD
LORA_DECAY = 64
LORA_AAA = 64
LORA_GATE = 160
RWKV_GN_EPS = 64e-5
N_SHIFT = 6

RET_HEADS = 4
RET_QK = D_MODEL // RET_HEADS
RET_V = 2 * RET_QK
RET_CHUNK = 128
ROPE_BASE = 10000.0

D_FF = 2816
N_EXPERTS = 8
TOP_K = 2
D_FF_EXPERT = 3584

kernel_name = "hybrid_rwkv7_retnet_moe"


def rmsnorm(x, g):
    xf = x.astype(jnp.float32)
    y = xf * lax.rsqrt(jnp.mean(xf * xf, axis=-1, keepdims=True) + RMS_EPS)
    return (y * g.astype(jnp.float32)).astype(x.dtype)


def token_shift(x):
    return jnp.pad(x, ((0, 0), (1, 0), (0, 0)))[:, :-1]


def rwkv7_time_mix(h, mu, wr, wk, wv, wo, w0, w1, w2, a0, a1, a2, g1, g2,
                   k_k, k_a, r_k, gn_w, gn_b):
    B, T, D = h.shape
    H, N = RWKV_HEADS, RWKV_HEAD
    f32 = jnp.float32
    dx = token_shift(h) - h
    xr = h + dx * mu[0]
    xw = h + dx * mu[1]
    xk = h + dx * mu[2]
    xv = h + dx * mu[3]
    xa = h + dx * mu[4]
    xg = h + dx * mu[5]
    r = xr @ wr
    k = xk @ wk
    v = xv @ wv
    w = -jax.nn.softplus(-(w0 + jnp.tanh(xw @ w1) @ w2)) - 0.5
    a = jax.nn.sigmoid(a0 + (xa @ a1) @ a2)
    g = jax.nn.sigmoid(xg @ g1) @ g2

    kk = (k * k_k).reshape(B, T, H, N).astype(f32)
    kk = kk / jnp.maximum(jnp.sqrt(jnp.sum(kk * kk, axis=-1, keepdims=True)), 1e-12)
    k = k * (1.0 + (a - 1.0) * k_a)

    def heads_tm(t):
        return t.astype(f32).reshape(B, T, H, N).transpose(1, 0, 2, 3)

    decay = jnp.exp(-jnp.exp(w.astype(f32)))
    a_h = heads_tm(a)
    kk_t = kk.transpose(1, 0, 2, 3)
    xs = (heads_tm(r), heads_tm(decay), heads_tm(k), heads_tm(v), -kk_t, kk_t * a_h)

    def step(S, inp):
        r_t, d_t, k_t, v_t, aa_t, bb_t = inp
        sa = jnp.einsum('bhvk,bhk->bhv', S, aa_t)
        S = S * d_t[:, :, None, :] + sa[..., :, None] * bb_t[..., None, :] \
            + v_t[..., :, None] * k_t[..., None, :]
        y = jnp.einsum('bhvk,bhk->bhv', S, r_t)
        return S, y

    S0 = jnp.zeros((B, H, N, N), f32)
    _, y = lax.scan(step, S0, xs)
    y = y.transpose(1, 0, 2, 3)

    mean = jnp.mean(y, axis=-1, keepdims=True)
    var = jnp.mean(jnp.square(y - mean), axis=-1, keepdims=True)
    y = ((y - mean) * lax.rsqrt(var + RWKV_GN_EPS)).reshape(B, T, D)
    y = y * gn_w.astype(f32) + gn_b.astype(f32)
    rh = r.astype(f32).reshape(B, T, H, N)
    kh = k.astype(f32).reshape(B, T, H, N)
    vh = v.astype(f32).reshape(B, T, H, N)
    bonus = jnp.sum(rh * kh * r_k.astype(f32), axis=-1, keepdims=True) * vh
    y = y + bonus.reshape(B, T, D)
    return ((y.astype(h.dtype) * g) @ wo).astype(h.dtype)


def rotate_every_two(t):
    t1 = t[..., ::2]
    t2 = t[..., 1::2]
    return jnp.stack((-t2, t1), axis=-1).reshape(t.shape)


def retention_rotation(t):
    T = t.shape[1]
    angle = 1.0 / (ROPE_BASE ** jnp.linspace(0.0, 1.0, RET_QK // 2, dtype=jnp.float32))
    angle = jnp.repeat(angle, 2)
    pos = jnp.arange(T, dtype=jnp.float32)
    ang = pos[:, None] * angle[None, :]
    sin = jnp.sin(ang)[None, :, None, :]
    cos = jnp.cos(ang)[None, :, None, :]
    return t * cos + rotate_every_two(t) * sin


def retention_mix(h, wq, wk, wv, wg, wo):
    B, T, D = h.shape
    H, dk, dv, C = RET_HEADS, RET_QK, RET_V, RET_CHUNK
    NC = T // C
    f32 = jnp.float32
    q = (h @ wq).astype(f32).reshape(B, T, H, dk)
    k = (h @ wk).astype(f32).reshape(B, T, H, dk) * (dk ** -0.5)
    v = (h @ wv).astype(f32).reshape(B, T, H, dv)
    g = h @ wg
    q = retention_rotation(q)
    k = retention_rotation(k)

    lg = jnp.log(1.0 - 2.0 ** (-5.0 - jnp.arange(H, dtype=f32)))
    n = jnp.arange(C, dtype=f32)
    diff = n[:, None] - n[None, :]
    intra = jnp.where(diff >= 0, jnp.exp(lg[:, None, None] * jnp.maximum(diff, 0.0)), 0.0)
    q_decay = jnp.exp(lg[:, None] * (n[None, :] + 1.0))
    k_decay = jnp.exp(lg[:, None] * (C - 1.0 - n[None, :]))
    chunk_decay = jnp.exp(lg * C)

    def chunks(t):
        return t.reshape(B, NC, C, H, t.shape[-1]).transpose(1, 0, 3, 2, 4)

    def step(R, inp):
        qc, kc, vc = inp
        s = jnp.einsum('bhcd,bhmd->bhcm', qc, kc) * intra[None]
        o = jnp.einsum('bhcm,bhme->bhce', s, vc) \
            + jnp.einsum('bhcd,bhde->bhce', qc, R) * q_decay[None, :, :, None]
        R = R * chunk_decay[None, :, None, None] \
            + jnp.einsum('bhmd,bhme->bhde', kc * k_decay[None, :, :, None], vc)
        return R, o

    R0 = jnp.zeros((B, H, dk, dv), f32)
    _, o = lax.scan(step, R0, (chunks(q), chunks(k), chunks(v)))
    o = o.transpose(1, 0, 3, 2, 4).reshape(B, T, H, dv)
    o = o * lax.rsqrt(jnp.mean(o * o, axis=-1, keepdims=True) + RMS_EPS)
    o = o.reshape(B, T, H * dv).astype(h.dtype)
    return ((jax.nn.silu(g) * o) @ wo).astype(h.dtype)


def swiglu(x, w1, w3, w2):
    return (jax.nn.silu(x @ w1) * (x @ w3)) @ w2


def moe_swiglu(h, router, w1, w3, w2):
    B, T, D = h.shape
    xt = h.reshape(B * T, D)
    logits = (xt @ router).astype(jnp.float32)
    top_val, top_idx = lax.top_k(logits, TOP_K)
    gates = jax.nn.softmax(top_val, axis=-1)
    combine = jnp.sum(jax.nn.one_hot(top_idx, N_EXPERTS, dtype=jnp.float32) * gates[..., None], axis=1)
    out = jnp.zeros((B * T, D), jnp.float32)
    for e in range(N_EXPERTS):
        out = out + combine[:, e:e + 1] * swiglu(xt, w1[e], w3[e], w2[e]).astype(jnp.float32)
    return out.astype(h.dtype).reshape(B, T, D)


def setup_inputs(seed: int = 0) -> dict:
    key = jax.random.key(seed)
    ks = iter(jax.random.split(key, 48))
    D = D_MODEL
    NA, NB = N_A_LAYERS, N_B_LAYERS
    H, N = RWKV_HEADS, RWKV_HEAD

    def nrm(shape, scale):
        return jax.random.normal(next(ks), shape, jnp.float32) * scale

    def uni(shape, lo, hi):
        return jax.random.uniform(next(ks), shape, jnp.float32, lo, hi)

    inp = {}
    inp["x"] = nrm((BATCH, SEQ, D), 1.0)
    inp["norm_mix"] = 1.0 + nrm((DEPTH, D), 0.02)
    inp["norm_ffn"] = 1.0 + nrm((DEPTH, D), 0.02)
    inp["norm_final"] = 1.0 + nrm((D,), 0.02)
    inp["a_mu"] = uni((NA, N_SHIFT, D), 0.0, 1.0)
    inp["a_wr"] = nrm((NA, D, D), D ** -0.5)
    inp["a_wk"] = nrm((NA, D, D), D ** -0.5)
    inp["a_wv"] = nrm((NA, D, D), D ** -0.5)
    inp["a_wo"] = nrm((NA, D, D), D ** -0.5)
    inp["a_w0"] = uni((NA, D), -6.0, 1.0)
    inp["a_w1"] = nrm((NA, D, LORA_DECAY), D ** -0.5)
    inp["a_w2"] = nrm((NA, LORA_DECAY, D), 0.1 * LORA_DECAY ** -0.5)
    inp["a_a0"] = nrm((NA, D), 0.1)
    inp["a_a1"] = nrm((NA, D, LORA_AAA), D ** -0.5)
    inp["a_a2"] = nrm((NA, LORA_AAA, D), 0.1 * LORA_AAA ** -0.5)
    inp["a_g1"] = nrm((NA, D, LORA_GATE), D ** -0.5)
    inp["a_g2"] = nrm((NA, LORA_GATE, D), LORA_GATE ** -0.5)
    inp["a_kk"] = 0.85 + nrm((NA, D), 0.05)
    inp["a_ka"] = 1.0 + nrm((NA, D), 0.05)
    inp["a_rk"] = nrm((NA, H, N), 0.1)
    inp["a_gn_w"] = 1.0 + nrm((NA, D), 0.02)
    inp["a_gn_b"] = nrm((NA, D), 0.02)
    inp["b_wq"] = nrm((NB, D, RET_HEADS * RET_QK), D ** -0.5)
    inp["b_wk"] = nrm((NB, D, RET_HEADS * RET_QK), D ** -0.5)
    inp["b_wv"] = nrm((NB, D, RET_HEADS * RET_V), D ** -0.5)
    inp["b_wg"] = nrm((NB, D, RET_HEADS * RET_V), D ** -0.5)
    inp["b_wo"] = nrm((NB, RET_HEADS * RET_V, D), (RET_HEADS * RET_V) ** -0.5)
    inp["f_w1"] = nrm((NA, D, D_FF), D ** -0.5)
    inp["f_w3"] = nrm((NA, D, D_FF), D ** -0.5)
    inp["f_w2"] = nrm((NA, D_FF, D), D_FF ** -0.5)
    inp["m_router"] = nrm((NB, D, N_EXPERTS), D ** -0.5)
    inp["m_w1"] = nrm((NB, N_EXPERTS, D, D_FF_EXPERT), D ** -0.5)
    inp["m_w3"] = nrm((NB, N_EXPERTS, D, D_FF_EXPERT), D ** -0.5)
    inp["m_w2"] = nrm((NB, N_EXPERTS, D_FF_EXPERT, D), D_FF_EXPERT ** -0.5)
    return inp


def reference(x, norm_mix, norm_ffn, norm_final,
              a_mu, a_wr, a_wk, a_wv, a_wo, a_w0, a_w1, a_w2, a_a0, a_a1, a_a2,
              a_g1, a_g2, a_kk, a_ka, a_rk, a_gn_w, a_gn_b,
              b_wq, b_wk, b_wv, b_wg, b_wo,
              f_w1, f_w3, f_w2,
              m_router, m_w1, m_w3, m_w2):
    h = x
    for i in range(DEPTH):
        j = i // N_MIXERS
        u = rmsnorm(h, norm_mix[i])
        if i % N_MIXERS == 0:
            h = h + rwkv7_time_mix(u, a_mu[j], a_wr[j], a_wk[j], a_wv[j], a_wo[j],
                                   a_w0[j], a_w1[j], a_w2[j], a_a0[j], a_a1[j], a_a2[j],
                                   a_g1[j], a_g2[j], a_kk[j], a_ka[j], a_rk[j],
                                   a_gn_w[j], a_gn_b[j])
        else:
            h = h + retention_mix(u, b_wq[j], b_wk[j], b_wv[j], b_wg[j], b_wo[j])
        u = rmsnorm(h, norm_ffn[i])
        c = i // 2
        if i % 2 == 0:
            h = h + swiglu(u, f_w1[c], f_w3[c], f_w2[c]).astype(h.dtype)
        else:
            h = h + moe_swiglu(u, m_router[c], m_w1[c], m_w3[c], m_w2[c])
    return rmsnorm(h, norm_final)
```

```python
import functools
import math

import jax
import jax.numpy as jnp
from jax import lax
from jax.experimental import pallas as pl
from jax.experimental.pallas import tpu as pltpu

F32 = jnp.float32
BF16 = jnp.bfloat16

RMS_EPS = 1e-6
RWKV_HEAD = 64
RWKV_GN_EPS = 64e-5
RET_HEADS = 4
ROPE_BASE = 10000.0
N_EXPERTS = 8
TOP_K = 2

LANES = 128
VMEM_LIMIT = 56 * 1024 * 1024

RWKV_CHUNK = 64
RWKV_SUB = 16
RET_BLOCK = 128
MOE_TILE = 512


def _cparams(sem):
    return pltpu.CompilerParams(dimension_semantics=sem, vmem_limit_bytes=VMEM_LIMIT)


def _dot(a, b):
    return jnp.dot(a.astype(BF16), b.astype(BF16), preferred_element_type=F32)


def _dot_nt(a, b):
    return lax.dot_general(a.astype(BF16), b.astype(BF16), (((1,), (1,)), ((), ())),
                           preferred_element_type=F32)


def _dot_tn(a, b):
    return lax.dot_general(a.astype(BF16), b.astype(BF16), (((0,), (0,)), ((), ())),
                           preferred_element_type=F32)


def _rms(x, g):
    return x * lax.rsqrt(jnp.mean(x * x, axis=-1, keepdims=True) + RMS_EPS) * g


def _sigmoid(x):
    return 1.0 / (1.0 + jnp.exp(-x))


def _silu(x):
    return x * _sigmoid(x)


def _full(shape):
    n = len(shape)
    return pl.BlockSpec(shape, lambda *_: (0,) * n)


def _rwkv_proj_kernel(T, tm, h_ref, hp_ref, nw_ref, mu_ref, wr_ref, wk_ref, wv_ref, w1_ref, w2_ref,
                      a1_ref, a2_ref, g1_ref, g2_ref, w0_ref, a0_ref,
                      r_out, ld_out, k_out, v_out, a_out, g_out):
    i = pl.program_id(0)
    nw = nw_ref[...]
    u = _rms(h_ref[...], nw)
    up = _rms(hp_ref[...], nw)
    first = (i * tm) % T == 0
    prev_last = jnp.where(first, 0.0, up[7:8, :])
    row = lax.broadcasted_iota(jnp.int32, u.shape, 0)
    us = jnp.where(row == 0, prev_last, pltpu.roll(u, 1, 0))
    dx = us - u

    def mix(j):
        return (u + dx * mu_ref[j:j + 1, :]).astype(BF16)

    r_out[...] = _dot(mix(0), wr_ref[...])
    z = w0_ref[...] + _dot(jnp.tanh(_dot(mix(1), w1_ref[...])), w2_ref[...])
    nz = -z
    softplus = jnp.maximum(nz, 0.0) + jnp.log(1.0 + jnp.exp(-jnp.abs(nz)))
    ld_out[...] = -jnp.exp(-softplus - 0.5)
    k_out[...] = _dot(mix(2), wk_ref[...])
    v_out[...] = _dot(mix(3), wv_ref[...])
    a_out[...] = _sigmoid(a0_ref[...] + _dot(_dot(mix(4), a1_ref[...]), a2_ref[...]))
    g_out[...] = _dot(_sigmoid(_dot(mix(5), g1_ref[...])), g2_ref[...])


def _rwkv_proj(h, T, nw, mu, wr, wk, wv, w1, w2, a1, a2, g1, g2, w0, a0, tm=256):
    N, D = h.shape
    tm = min(tm, T)
    row = pl.BlockSpec((tm, D), lambda i: (i, 0))
    prev = pl.BlockSpec((8, D), lambda i: (jnp.maximum(i * (tm // 8) - 1, 0), 0))
    ws = [nw, mu, wr, wk, wv, w1, w2, a1, a2, g1, g2, w0, a0]
    out = jax.ShapeDtypeStruct((N, D), F32)
    return pl.pallas_call(
        functools.partial(_rwkv_proj_kernel, T, tm),
        grid=(N // tm,),
        in_specs=[row, prev] + [_full(w.shape) for w in ws],
        out_specs=[row] * 6,
        out_shape=[out] * 6,
        compiler_params=_cparams(("parallel",)),
        name="rwkv_proj",
    )(h, h, *ws)


def _rwkv_pair_chunk(r, ld, k, v, a, kkp, kap, rkp, gnw, gnb, s_ref):
    L = r.shape[0]
    L2 = 2 * L
    lane = lax.broadcasted_iota(jnp.int32, (L, LANES), 1)
    m0 = lane < RWKV_HEAD

    def seg_sum(x):
        s0 = jnp.sum(jnp.where(m0, x, 0.0), axis=-1, keepdims=True)
        s1 = jnp.sum(jnp.where(m0, 0.0, x), axis=-1, keepdims=True)
        return jnp.where(m0, s0, s1)

    def stack(x):
        return jnp.concatenate([jnp.where(m0, x, 0.0), jnp.where(m0, 0.0, x)], axis=0)

    kk = k * kkp
    kk = kk / jnp.maximum(jnp.sqrt(seg_sum(kk * kk)), 1e-12)
    kmod = k * (1.0 + (a - 1.0) * kap)
    aa = -kk
    bb = kk * a

    ti = lax.broadcasted_iota(jnp.int32, (L, L), 0)
    tj = lax.broadcasted_iota(jnp.int32, (L, L), 1)
    tri = jnp.where(tj <= ti, 1.0, 0.0).astype(F32)
    cum = jnp.dot(tri, ld, precision=lax.Precision.HIGHEST, preferred_element_type=F32)
    tot = cum[L - 1:L, :]
    e_cum = jnp.exp(cum)
    e_inv = jnp.exp(-cum)
    e_fin = jnp.exp(tot - cum)
    a_t = stack(aa * jnp.exp(cum - ld))
    r_t = stack(r * e_cum)
    b_t = stack(bb * e_inv)
    k_t = stack(kmod * e_inv)
    b_h = stack(bb * e_fin)
    k_h = stack(kmod * e_fin)
    v_s = stack(v)

    mm = _dot_nt(jnp.concatenate([a_t, r_t], axis=0), jnp.concatenate([b_t, k_t], axis=0))
    si = lax.broadcasted_iota(jnp.int32, (L2, L2), 0)
    sj = lax.broadcasted_iota(jnp.int32, (L2, L2), 1)
    low = sj < si
    lowi = sj <= si
    m_ab = jnp.where(low, mm[:L2, :L2], 0.0)
    m_ak = jnp.where(low, mm[:L2, L2:], 0.0)
    m_rb = jnp.where(lowi, mm[L2:, :L2], 0.0)
    m_rk = jnp.where(lowi, mm[L2:, L2:], 0.0)

    eye = jnp.where(si == sj, 1.0, 0.0).astype(F32)
    sub_shift = RWKV_SUB.bit_length() - 1
    same = (si >> sub_shift) == (sj >> sub_shift)
    dg = jnp.where(same, m_ab, 0.0)
    og = jnp.where(same, 0.0, m_ab)
    d2 = _dot(dg, dg)
    d4 = _dot(d2, d2)
    d8 = _dot(d4, d4)
    x1 = eye + dg + d2 + _dot(dg, d2)
    x2 = eye + d4 + d8 + _dot(d4, d8)
    t_d = _dot(x1, x2)
    e1 = _dot(t_d, og)
    e2 = _dot(e1, e1)
    t_m = _dot(eye + e1 + e2 + _dot(e1, e2), t_d)

    s0 = s_ref[...]
    c = _dot_nt(a_t, s0) + _dot(m_ak, v_s)
    u = _dot(t_m, c)
    ys = _dot_nt(r_t, s0) + _dot(m_rb, u) + _dot(m_rk, v_s)
    y = ys[:L] + ys[L:]
    s_ref[...] = s0 * jnp.exp(tot) + _dot_tn(jnp.concatenate([u, v_s], axis=0),
                                            jnp.concatenate([b_h, k_h], axis=0))

    inv_n = 1.0 / RWKV_HEAD
    mean = seg_sum(y) * inv_n
    yc = y - mean
    var = seg_sum(yc * yc) * inv_n
    yn = yc * lax.rsqrt(var + RWKV_GN_EPS) * gnw + gnb
    return yn + seg_sum(r * kmod * rkp) * v


def _rwkv_recur_kernel(G, r_ref, ld_ref, k_ref, v_ref, a_ref, kk_ref, ka_ref, rk_ref, gw_ref, gb_ref,
                       y_ref, s_ref):
    @pl.when(pl.program_id(2) == 0)
    def _():
        s_ref[...] = jnp.zeros_like(s_ref)

    for g in range(G):
        sl = slice(g * LANES, (g + 1) * LANES)
        y_ref[:, sl] = _rwkv_pair_chunk(
            r_ref[:, sl], ld_ref[:, sl], k_ref[:, sl], v_ref[:, sl], a_ref[:, sl],
            kk_ref[:, sl], ka_ref[:, sl], rk_ref[:, sl], gw_ref[:, sl], gb_ref[:, sl], s_ref.at[g])


def _rwkv_recur(r, ld, k, v, a, kkp, kap, rkp, gnw, gnb, B, T, G=4):
    N, D = r.shape
    L = RWKV_CHUNK
    W = G * LANES
    nc = T // L
    seq = pl.BlockSpec((L, W), lambda b, p, c: (b * nc + c, p))
    par = pl.BlockSpec((1, W), lambda b, p, c: (0, p))
    return pl.pallas_call(
        functools.partial(_rwkv_recur_kernel, G),
        grid=(B, D // W, nc),
        in_specs=[seq] * 5 + [par] * 5,
        out_specs=seq,
        out_shape=jax.ShapeDtypeStruct((N, D), F32),
        scratch_shapes=[pltpu.VMEM((G, LANES, LANES), F32)],
        compiler_params=_cparams(("parallel", "parallel", "arbitrary")),
        name="rwkv_recur",
    )(r, ld, k, v, a, kkp, kap, rkp, gnw, gnb)


def _gated_out_kernel(h_ref, y_ref, g_ref, wo_ref, o_ref):
    o_ref[...] = h_ref[...] + _dot(y_ref[...].astype(F32) * g_ref[...].astype(F32), wo_ref[...])


def _gated_out(h, y, g, wo, tm=512):
    N, D = h.shape
    K = y.shape[1]
    tm = min(tm, N)
    return pl.pallas_call(
        _gated_out_kernel,
        grid=(N // tm,),
        in_specs=[pl.BlockSpec((tm, D), lambda i: (i, 0)), pl.BlockSpec((tm, K), lambda i: (i, 0)),
                  pl.BlockSpec((tm, K), lambda i: (i, 0)), _full(wo.shape)],
        out_specs=pl.BlockSpec((tm, D), lambda i: (i, 0)),
        out_shape=jax.ShapeDtypeStruct((N, D), F32),
        compiler_params=_cparams(("parallel",)),
        name="gated_out",
    )(h, y, g, wo)


def _swiglu_chunk(u, w1, w3, w2):
    return _dot(_silu(_dot(u, w1)) * _dot(u, w3), w2)


def _dense_ffn_kernel(h_ref, nw_ref, w1_ref, w3_ref, w2_ref, o_ref, u_ref, acc_ref):
    f = pl.program_id(1)

    @pl.when(f == 0)
    def _():
        h = h_ref[...]
        u_ref[...] = _rms(h, nw_ref[...]).astype(BF16)
        acc_ref[...] = h

    acc_ref[...] += _swiglu_chunk(u_ref[...], w1_ref[...], w3_ref[...], w2_ref[...])

    @pl.when(f == pl.num_programs(1) - 1)
    def _():
        o_ref[...] = acc_ref[...]


def _dense_ffn(h, nw, w1, w3, w2, tm=512, tf=256):
    N, D = h.shape
    Fd = w1.shape[1]
    tm = min(tm, N)
    return pl.pallas_call(
        _dense_ffn_kernel,
        grid=(N // tm, Fd // tf),
        in_specs=[pl.BlockSpec((tm, D), lambda i, f: (i, 0)), _full(nw.shape),
                  pl.BlockSpec((D, tf), lambda i, f: (0, f)), pl.BlockSpec((D, tf), lambda i, f: (0, f)),
                  pl.BlockSpec((tf, D), lambda i, f: (f, 0))],
        out_specs=pl.BlockSpec((tm, D), lambda i, f: (i, 0)),
        out_shape=jax.ShapeDtypeStruct((N, D), F32),
        scratch_shapes=[pltpu.VMEM((tm, D), BF16), pltpu.VMEM((tm, D), F32)],
        compiler_params=_cparams(("parallel", "arbitrary")),
        name="dense_ffn",
    )(h, nw, w1, w3, w2)


def _ret_proj_kernel(dk, h_ref, nw_ref, cos_ref, sin_ref, wq_ref, wk_ref, wv_ref, wg_ref,
                     q_out, k_out, v_out, g_out):
    u = _rms(h_ref[...], nw_ref[...]).astype(BF16)
    nh = wq_ref.shape[1] // dk
    cos = jnp.concatenate([cos_ref[...]] * nh, axis=1)
    sin = jnp.concatenate([sin_ref[...]] * nh, axis=1)
    lane = lax.broadcasted_iota(jnp.int32, cos.shape, 1)
    even = (lane & 1) == 0
    width = cos.shape[1]

    def rope(t):
        nxt = pltpu.roll(t, width - 1, 1)
        prv = pltpu.roll(t, 1, 1)
        return t * cos + jnp.where(even, nxt, prv) * sin

    q_out[...] = rope(_dot(u, wq_ref[...])).astype(BF16)
    k_out[...] = rope(_dot(u, wk_ref[...]) * (dk ** -0.5)).astype(BF16)
    v_out[...] = _dot(u, wv_ref[...]).astype(BF16)
    g_out[...] = _dot(u, wg_ref[...]).astype(BF16)


def _ret_proj(h, T, nw, cos, sin, wq, wk, wv, wg, tm=256):
    N, D = h.shape
    dk = cos.shape[1]
    tm = min(tm, T)
    tpb = T // tm
    row = lambda w: pl.BlockSpec((tm, w), lambda i: (i, 0))
    tab = pl.BlockSpec((tm, dk), lambda i: (i % tpb, 0))
    return pl.pallas_call(
        functools.partial(_ret_proj_kernel, dk),
        grid=(N // tm,),
        in_specs=[row(D), _full(nw.shape), tab, tab] + [_full(w.shape) for w in (wq, wk, wv, wg)],
        out_specs=[row(wq.shape[1]), row(wk.shape[1]), row(wv.shape[1]), row(wg.shape[1])],
        out_shape=[jax.ShapeDtypeStruct((N, w.shape[1]), BF16) for w in (wq, wk, wv, wg)],
        compiler_params=_cparams(("parallel",)),
        name="ret_proj",
    )(h, nw, cos, sin, wq, wk, wv, wg)


def _ret_chunk_kernel(q_ref, k_ref, v_ref, g_ref, intra_ref, qd_ref, kd_ref, cd_ref, o_ref, r_ref):
    @pl.when(pl.program_id(2) == 0)
    def _():
        r_ref[...] = jnp.zeros_like(r_ref)

    q = q_ref[...]
    k = k_ref[...]
    v = v_ref[...]
    r0 = r_ref[...]
    s = _dot_nt(q, k) * intra_ref[0]
    o = _dot(s, v) + _dot(q, r0) * qd_ref[0]
    r_ref[...] = r0 * cd_ref[0] + _dot_tn(k.astype(F32) * kd_ref[0], v)
    o = o * lax.rsqrt(jnp.mean(o * o, axis=-1, keepdims=True) + RMS_EPS)
    o_ref[...] = (_silu(g_ref[...].astype(F32)) * o).astype(BF16)


def _ret_chunk(q, k, v, g, B, T, H):
    N = q.shape[0]
    dk = q.shape[1] // H
    dv = v.shape[1] // H
    C = min(RET_BLOCK, T)
    nc = T // C
    lg = jnp.log(1.0 - 2.0 ** (-5.0 - jnp.arange(H, dtype=F32)))
    n = jnp.arange(C, dtype=F32)
    diff = n[:, None] - n[None, :]
    intra = jnp.where(diff >= 0, jnp.exp(lg[:, None, None] * jnp.maximum(diff, 0.0)), 0.0)
    qd = jnp.broadcast_to(jnp.exp(lg[:, None] * (n[None, :] + 1.0))[:, :, None], (H, C, LANES))
    kd = jnp.broadcast_to(jnp.exp(lg[:, None] * (C - 1.0 - n[None, :]))[:, :, None], (H, C, LANES))
    cd = jnp.broadcast_to(jnp.exp(lg * C)[:, None, None], (H, 8, LANES))
    qd = jnp.concatenate([qd] * (dv // LANES), axis=2)
    kd = jnp.concatenate([kd] * (dk // LANES), axis=2)
    cd = jnp.concatenate([cd] * (dv // LANES), axis=2)[:, :1]
    seq = lambda w: pl.BlockSpec((C, w), lambda b, h, c: (b * nc + c, h))
    hd = lambda s: pl.BlockSpec((1,) + s, lambda b, h, c: (h, 0, 0))
    return pl.pallas_call(
        _ret_chunk_kernel,
        grid=(B, H, nc),
        in_specs=[seq(dk), seq(dk), seq(dv), seq(dv), hd((C, C)), hd((C, dv)), hd((C, dk)), hd((1, dv))],
        out_specs=seq(dv),
        out_shape=jax.ShapeDtypeStruct((N, H * dv), BF16),
        scratch_shapes=[pltpu.VMEM((dk, dv), F32)],
        compiler_params=_cparams(("parallel", "parallel", "arbitrary")),
        name="ret_chunk",
    )(q, k, v, g, intra, qd, kd, cd)


def _ret_out_router_kernel(h_ref, o_ref, wo_ref, nw_ref, rt_ref, h_out, u_out, idx_out, gate_out):
    h = h_ref[...] + _dot(o_ref[...], wo_ref[...])
    h_out[...] = h
    u = _rms(h, nw_ref[...])
    u_out[...] = u
    logits = lax.dot_general(rt_ref[...], u, (((1,), (1,)), ((), ())),
                             precision=lax.Precision.HIGHEST, preferred_element_type=F32)
    e_iota = lax.broadcasted_iota(jnp.int32, logits.shape, 0)
    big = jnp.int32(N_EXPERTS)
    m1 = jnp.max(logits, axis=0, keepdims=True)
    i1 = jnp.min(jnp.where(logits == m1, e_iota, big), axis=0, keepdims=True)
    rest = jnp.where(e_iota == i1, -jnp.inf, logits)
    m2 = jnp.max(rest, axis=0, keepdims=True)
    i2 = jnp.min(jnp.where(rest == m2, e_iota, big), axis=0, keepdims=True)
    p2 = jnp.exp(m2 - m1)
    g1 = 1.0 / (1.0 + p2)
    idx_out[...] = jnp.concatenate([i1, i2], axis=0)
    gate_out[...] = jnp.concatenate([g1, p2 * g1], axis=0)


def _ret_out_router(h, o, wo, nw, router_t, tm=512):
    N, D = h.shape
    K = o.shape[1]
    tm = min(tm, N)
    row = lambda w: pl.BlockSpec((tm, w), lambda i: (i, 0))
    col = pl.BlockSpec((TOP_K, tm), lambda i: (0, i))
    return pl.pallas_call(
        _ret_out_router_kernel,
        grid=(N // tm,),
        in_specs=[row(D), row(K), _full(wo.shape), _full(nw.shape), _full(router_t.shape)],
        out_specs=[row(D), row(D), col, col],
        out_shape=[jax.ShapeDtypeStruct((N, D), F32), jax.ShapeDtypeStruct((N, D), F32),
                   jax.ShapeDtypeStruct((TOP_K, N), jnp.int32), jax.ShapeDtypeStruct((TOP_K, N), F32)],
        compiler_params=_cparams(("parallel",)),
        name="ret_out_router",
    )(h, o, wo, nw, router_t)


def _row_copy(src_hbm, dst_ref, sem, s, j):
    return pltpu.make_async_copy(src_hbm.at[pl.ds(s, 1), :], dst_ref.at[pl.ds(j, 1), :], sem)


def _gather_rows_kernel(tg, src_ref, u_hbm, o_ref, sem):
    def issue(j, c):
        _row_copy(u_hbm, o_ref, sem, src_ref[j], j).start()
        return c

    lax.fori_loop(0, tg, issue, 0)

    def drain(j, c):
        _row_copy(u_hbm, o_ref, sem, 0, j).wait()
        return c

    lax.fori_loop(0, tg, drain, 0)


def _gather_rows(u, src, tg=512):
    R = src.shape[0]
    D = u.shape[1]
    return pl.pallas_call(
        functools.partial(_gather_rows_kernel, tg),
        grid=(R // tg,),
        in_specs=[pl.BlockSpec((tg,), lambda i: (i,), memory_space=pltpu.SMEM),
                  pl.BlockSpec(memory_space=pl.ANY)],
        out_specs=pl.BlockSpec((tg, D), lambda i: (i, 0)),
        out_shape=jax.ShapeDtypeStruct((R, D), u.dtype),
        scratch_shapes=[pltpu.SemaphoreType.DMA(())],
        compiler_params=_cparams(("arbitrary",)),
        name="moe_gather",
    )(src, u)


def _expert_kernel(te_ref, nv_ref, x_ref, w1_ref, w3_ref, w2_ref, o_ref, xb_ref, acc_ref):
    i = pl.program_id(0)
    f = pl.program_id(1)
    valid = i < nv_ref[0]

    @pl.when(valid)
    def _():
        @pl.when(f == 0)
        def _():
            xb_ref[...] = x_ref[...].astype(BF16)
            acc_ref[...] = jnp.zeros_like(acc_ref)

        acc_ref[...] += _swiglu_chunk(xb_ref[...], w1_ref[0], w3_ref[0], w2_ref[0])

    @pl.when(f == pl.num_programs(1) - 1)
    def _():
        o_ref[...] = jnp.where(valid, acc_ref[...], 0.0)


def _experts(xs, tile_expert, n_valid, w1, w3, w2, tf=896):
    R, D = xs.shape
    Fd = w1.shape[2]
    tmm = MOE_TILE
    nf = Fd // tf

    def w13_map(i, f, te, nv):
        return (te[i], 0, jnp.where(i < nv[0], f, nf - 1))

    def w2_map(i, f, te, nv):
        return (te[i], jnp.where(i < nv[0], f, nf - 1), 0)

    gs = pltpu.PrefetchScalarGridSpec(
        num_scalar_prefetch=2,
        grid=(R // tmm, nf),
        in_specs=[pl.BlockSpec((tmm, D), lambda i, f, te, nv: (i, 0)),
                  pl.BlockSpec((1, D, tf), w13_map), pl.BlockSpec((1, D, tf), w13_map),
                  pl.BlockSpec((1, tf, D), w2_map)],
        out_specs=pl.BlockSpec((tmm, D), lambda i, f, te, nv: (i, 0)),
        scratch_shapes=[pltpu.VMEM((tmm, D), BF16), pltpu.VMEM((tmm, D), F32)],
    )
    return pl.pallas_call(
        _expert_kernel,
        grid_spec=gs,
        out_shape=jax.ShapeDtypeStruct((R, D), F32),
        compiler_params=_cparams(("arbitrary", "arbitrary")),
        name="moe_experts",
    )(tile_expert, n_valid, xs, w1, w3, w2)


def _combine_kernel(tc, pos_ref, h_ref, gate_ref, nw_ref, ys_hbm, o_ref, buf_ref, sem):
    def issue(j, c):
        _row_copy(ys_hbm, buf_ref.at[0], sem, pos_ref[0, j], j).start()
        _row_copy(ys_hbm, buf_ref.at[1], sem, pos_ref[1, j], j).start()
        return c

    lax.fori_loop(0, tc, issue, 0)

    def drain(j, c):
        _row_copy(ys_hbm, buf_ref.at[0], sem, 0, j).wait()
        _row_copy(ys_hbm, buf_ref.at[1], sem, 0, j).wait()
        return c

    lax.fori_loop(0, tc, drain, 0)
    gate = gate_ref[...]
    moe = gate[:, 0:1] * buf_ref[0] + gate[:, 1:2] * buf_ref[1]
    o_ref[...] = _rms(h_ref[...] + moe, nw_ref[...])


def _combine(h, ys, pos, gate, nw, tc=256):
    N, D = h.shape
    tc = min(tc, N)
    return pl.pallas_call(
        functools.partial(_combine_kernel, tc),
        grid=(N // tc,),
        in_specs=[pl.BlockSpec((TOP_K, tc), lambda i: (0, i), memory_space=pltpu.SMEM),
                  pl.BlockSpec((tc, D), lambda i: (i, 0)),
                  pl.BlockSpec((tc, TOP_K), lambda i: (i, 0)),
                  _full(nw.shape),
                  pl.BlockSpec(memory_space=pl.ANY)],
        out_specs=pl.BlockSpec((tc, D), lambda i: (i, 0)),
        out_shape=jax.ShapeDtypeStruct((N, D), F32),
        scratch_shapes=[pltpu.VMEM((TOP_K, tc, D), F32), pltpu.SemaphoreType.DMA(())],
        compiler_params=_cparams(("arbitrary",)),
        name="moe_combine",
    )(pos, h, gate, nw, ys)


def _route_plan(idx, N):
    tmm = MOE_TILE
    n_tiles = (TOP_K * N) // tmm + N_EXPERTS
    flat_e = idx.T.reshape(-1)
    onehot = (flat_e[:, None] == jnp.arange(N_EXPERTS, dtype=jnp.int32)[None, :]).astype(jnp.int32)
    csum = jnp.cumsum(onehot, axis=0)
    rank = jnp.sum(csum * onehot, axis=1) - 1
    counts = csum[-1]
    ptiles = (counts + tmm - 1) // tmm
    tile_end = jnp.cumsum(ptiles)
    tile_start = tile_end - ptiles
    pos = tile_start[flat_e] * tmm + rank
    token = jnp.arange(TOP_K * N, dtype=jnp.int32) // TOP_K
    src = jnp.zeros((n_tiles * tmm,), jnp.int32).at[pos].set(token)
    n_valid = tile_end[-1]
    t = jnp.minimum(jnp.arange(n_tiles, dtype=jnp.int32), n_valid - 1)
    tile_expert = jnp.sum((t[:, None] >= tile_end[None, :]).astype(jnp.int32), axis=1)
    return src, pos.reshape(N, TOP_K).T.astype(jnp.int32), tile_expert.astype(jnp.int32), \
        n_valid.reshape(1).astype(jnp.int32)


def _rope_tables(T, dk):
    angle = 1.0 / (ROPE_BASE ** jnp.linspace(0.0, 1.0, dk // 2, dtype=F32))
    angle = jnp.repeat(angle, 2)
    ang = jnp.arange(T, dtype=F32)[:, None] * angle[None, :]
    sign = jnp.where(jnp.arange(dk) % 2 == 0, -1.0, 1.0).astype(F32)
    return jnp.cos(ang), jnp.sin(ang) * sign[None, :]


def kernel(x, norm_mix, norm_ffn, norm_final, a_mu, a_wr, a_wk, a_wv, a_wo, a_w0, a_w1, a_w2, a_a0, a_a1, a_a2, a_g1, a_g2, a_kk, a_ka, a_rk, a_gn_w, a_gn_b, b_wq, b_wk, b_wv, b_wg, b_wo, f_w1, f_w3, f_w2, m_router, m_w1, m_w3, m_w2):
    B, T, D = x.shape
    N = B * T
    bf = lambda w: w.astype(BF16)
    vec = lambda w: w.reshape(1, -1).astype(F32)
    h = x.reshape(N, D)

    r, ld, k, v, a, g = _rwkv_proj(
        h, T, vec(norm_mix[0]), a_mu[0], bf(a_wr[0]), bf(a_wk[0]), bf(a_wv[0]), bf(a_w1[0]), bf(a_w2[0]),
        bf(a_a1[0]), bf(a_a2[0]), bf(a_g1[0]), bf(a_g2[0]), vec(a_w0[0]), vec(a_a0[0]))
    y = _rwkv_recur(r, ld, k, v, a, vec(a_kk[0]), vec(a_ka[0]), vec(a_rk[0]), vec(a_gn_w[0]),
                    vec(a_gn_b[0]), B, T)
    h = _gated_out(h, y, g, bf(a_wo[0]))
    h = _dense_ffn(h, vec(norm_ffn[0]), bf(f_w1[0]), bf(f_w3[0]), bf(f_w2[0]))

    dk = b_wq.shape[2] // RET_HEADS
    cos, sin = _rope_tables(T, dk)
    q, kr, vr, gr = _ret_proj(h, T, vec(norm_mix[1]), cos, sin, bf(b_wq[0]), bf(b_wk[0]), bf(b_wv[0]),
                              bf(b_wg[0]))
    o = _ret_chunk(q, kr, vr, gr, B, T, RET_HEADS)
    h, u, idx, gate = _ret_out_router(h, o, bf(b_wo[0]), vec(norm_ffn[1]), m_router[0].T.astype(F32))
    src, pos, tile_expert, n_valid = _route_plan(idx, N)
    xs = _gather_rows(u, src)
    ys = _experts(xs, tile_expert, n_valid, bf(m_w1[0]), bf(m_w3[0]), bf(m_w2[0]))
    out = _combine(h, ys, pos, gate.T, vec(norm_final))
    return out.reshape(B, T, D)
```

```python
import functools
import math

import jax
import jax.numpy as jnp
from jax import lax
from jax.experimental import pallas as pl
from jax.experimental.pallas import tpu as pltpu

F32 = jnp.float32
BF16 = jnp.bfloat16

RMS_EPS = 1e-6
RWKV_HEAD = 64
RWKV_GN_EPS = 64e-5
RET_HEADS = 4
ROPE_BASE = 10000.0
N_EXPERTS = 8
TOP_K = 2

LANES = 128
VMEM_LIMIT = 56 * 1024 * 1024

RWKV_CHUNK = 64
RWKV_SUB = 16
RET_BLOCK = 256
MOE_TILE = 512
DMA_UNROLL = 8


def _cparams(sem):
    return pltpu.CompilerParams(dimension_semantics=sem, vmem_limit_bytes=VMEM_LIMIT)


def _dot(a, b):
    return jnp.dot(a.astype(BF16), b.astype(BF16), preferred_element_type=F32)


def _dot_nt(a, b):
    return lax.dot_general(a.astype(BF16), b.astype(BF16), (((1,), (1,)), ((), ())),
                           preferred_element_type=F32)


def _dot_tn(a, b):
    return lax.dot_general(a.astype(BF16), b.astype(BF16), (((0,), (0,)), ((), ())),
                           preferred_element_type=F32)


def _rms(x, g):
    return x * lax.rsqrt(jnp.mean(x * x, axis=-1, keepdims=True) + RMS_EPS) * g


def _sigmoid(x):
    return 1.0 / (1.0 + jnp.exp(-x))


def _silu(x):
    return x * _sigmoid(x)


def _full(shape):
    n = len(shape)
    return pl.BlockSpec(shape, lambda *_: (0,) * n)


def _rwkv_proj_kernel(T, tm, h_ref, hp_ref, nw_ref, mu_ref, wr_ref, wk_ref, wv_ref, w1_ref, w2_ref,
                      a1_ref, a2_ref, g1_ref, g2_ref, w0_ref, a0_ref,
                      r_out, ld_out, k_out, v_out, a_out, g_out):
    i = pl.program_id(0)
    nw = nw_ref[...]
    u = _rms(h_ref[...], nw)
    up = _rms(hp_ref[...], nw)
    first = (i * tm) % T == 0
    prev_last = jnp.where(first, 0.0, up[7:8, :])
    row = lax.broadcasted_iota(jnp.int32, u.shape, 0)
    us = jnp.where(row == 0, prev_last, pltpu.roll(u, 1, 0))
    dx = us - u

    def mix(j):
        return (u + dx * mu_ref[j:j + 1, :]).astype(BF16)

    r_out[...] = _dot(mix(0), wr_ref[...]).astype(BF16)
    z =w0_ref[...] + _dot(jnp.tanh(_dot(mix(1), w1_ref[...])), w2_ref[...])
    nz = -z
    softplus = jnp.maximum(nz, 0.0) + jnp.log(1.0 + jnp.exp(-jnp.abs(nz)))
    ld_out[...] = -jnp.exp(-softplus - 0.5)
    k_out[...] = _dot(mix(2), wk_ref[...]).astype(BF16)
    v_out[...] = _dot(mix(3), wv_ref[...]).astype(BF16)
    a_out[...] = _sigmoid(a0_ref[...] + _dot(_dot(mix(4), a1_ref[...]), a2_ref[...])).astype(BF16)
    g_out[...] = _dot(_sigmoid(_dot(mix(5), g1_ref[...])), g2_ref[...]).astype(BF16)


def _rwkv_proj(h, T, nw, mu, wr, wk, wv, w1, w2, a1, a2, g1, g2, w0, a0, tm=256):
    N, D = h.shape
    tm = min(tm, T)
    row = pl.BlockSpec((tm, D), lambda i: (i, 0))
    prev = pl.BlockSpec((8, D), lambda i: (jnp.maximum(i * (tm // 8) - 1, 0), 0))
    ws = [nw, mu, wr, wk, wv, w1, w2, a1, a2, g1, g2, w0, a0]
    out = lambda dt: jax.ShapeDtypeStruct((N, D), dt)
    return pl.pallas_call(
        functools.partial(_rwkv_proj_kernel, T, tm),
        grid=(N // tm,),
        in_specs=[row, prev] + [_full(w.shape) for w in ws],
        out_specs=[row] * 6,
        out_shape=[out(BF16), out(F32), out(BF16), out(BF16), out(BF16), out(BF16)],
        compiler_params=_cparams(("parallel",)),
        name="rwkv_proj",
    )(h, h, *ws)


def _rwkv_pair_chunk(r, ld, k, v, a, kkp, kap, rkp, gnw, gnb, s_ref):
    L = r.shape[0]
    L2 = 2 * L
    lane = lax.broadcasted_iota(jnp.int32, (L, LANES), 1)
    m0 = lane < RWKV_HEAD

    def seg_sum(x):
        s0 = jnp.sum(jnp.where(m0, x, 0.0), axis=-1, keepdims=True)
        s1 = jnp.sum(jnp.where(m0, 0.0, x), axis=-1, keepdims=True)
        return jnp.where(m0, s0, s1)

    def stack(x):
        return jnp.concatenate([jnp.where(m0, x, 0.0), jnp.where(m0, 0.0, x)], axis=0)

    kk = k * kkp
    kk = kk / jnp.maximum(jnp.sqrt(seg_sum(kk * kk)), 1e-12)
    kmod = k * (1.0 + (a - 1.0) * kap)
    aa = -kk
    bb = kk * a

    ti = lax.broadcasted_iota(jnp.int32, (L, L), 0)
    tj = lax.broadcasted_iota(jnp.int32, (L, L), 1)
    tri = jnp.where(tj <= ti, 1.0, 0.0).astype(F32)
    cum = jnp.dot(tri, ld, precision=lax.Precision.HIGHEST, preferred_element_type=F32)
    yield
    tot = cum[L - 1:L, :]
    e_cum = jnp.exp(cum)
    e_inv = jnp.exp(-cum)
    e_fin = jnp.exp(tot - cum)
    a_t = stack(aa * jnp.exp(cum - ld))
    r_t = stack(r * e_cum)
    b_t = stack(bb * e_inv)
    k_t = stack(kmod * e_inv)
    b_h = stack(bb * e_fin)
    k_h = stack(kmod * e_fin)
    v_s = stack(v)

    mm = _dot_nt(jnp.concatenate([a_t, r_t], axis=0), jnp.concatenate([b_t, k_t], axis=0))
    yield
    si =lax.broadcasted_iota(jnp.int32, (L2, L2), 0)
    sj = lax.broadcasted_iota(jnp.int32, (L2, L2), 1)
    low = sj < si
    lowi = sj <= si
    m_ab = jnp.where(low, mm[:L2, :L2], 0.0)
    m_ak = jnp.where(low, mm[:L2, L2:], 0.0)
    m_rb = jnp.where(lowi, mm[L2:, :L2], 0.0)
    m_rk = jnp.where(lowi, mm[L2:, L2:], 0.0)

    eye = jnp.where(si == sj, 1.0, 0.0).astype(F32)
    sub_shift = RWKV_SUB.bit_length() - 1
    same = (si >> sub_shift) == (sj >> sub_shift)
    dg = jnp.where(same, m_ab, 0.0)
    og = jnp.where(same, 0.0, m_ab)
    d2 = _dot(dg, dg)
    s0 = s_ref[...]
    c = _dot_nt(a_t, s0) + _dot(m_ak, v_s)
    yield
    d4 = _dot(d2, d2)
    x1 = eye + dg + d2 + _dot(dg, d2)
    yield
    d8 = _dot(d4, d4)
    yield
    x2 = eye + d4 + d8 + _dot(d4, d8)
    yield
    t_d = _dot(x1, x2)
    yield
    e1 = _dot(t_d, og)
    yield
    e2 = _dot(e1, e1)
    yield
    t_f = eye + e1 + e2 + _dot(e1, e2)
    yield
    t_m = _dot(t_f, t_d)
    yield
    u = _dot(t_m, c)
    yield
    ys = _dot_nt(r_t, s0) + _dot(m_rb, u) + _dot(m_rk, v_s)
    s_ref[...] = s0 * jnp.exp(tot) + _dot_tn(jnp.concatenate([u, v_s], axis=0),
                                            jnp.concatenate([b_h, k_h], axis=0))
    yield
    y = ys[:L] + ys[L:]

    inv_n = 1.0 / RWKV_HEAD
    mean = seg_sum(y) * inv_n
    yc = y - mean
    var = seg_sum(yc * yc) * inv_n
    yn = yc * lax.rsqrt(var + RWKV_GN_EPS) * gnw + gnb
    return yn + seg_sum(r * kmod * rkp) * v


def _run_in_lockstep(gens):
    results = [None] * len(gens)
    live = list(range(len(gens)))
    while live:
        still = []
        for i in live:
            try:
                next(gens[i])
                still.append(i)
            except StopIteration as done:
                results[i] = done.value
        live = still
    return results


def _rwkv_recur_kernel(G, r_ref, ld_ref, k_ref, v_ref, a_ref, kk_ref, ka_ref, rk_ref, gw_ref, gb_ref,
                       y_ref, s_ref):
    @pl.when(pl.program_id(2) == 0)
    def _():
        s_ref[...] = jnp.zeros_like(s_ref)

    lanes = [slice(g * LANES, (g + 1) * LANES) for g in range(G)]
    f32 = lambda ref, sl: ref[:, sl].astype(F32)
    ys = _run_in_lockstep([
        _rwkv_pair_chunk(f32(r_ref, sl), ld_ref[:, sl], f32(k_ref, sl), f32(v_ref, sl), f32(a_ref, sl),
                         kk_ref[:, sl], ka_ref[:, sl], rk_ref[:, sl], gw_ref[:, sl], gb_ref[:, sl],
                         s_ref.at[g])
        for g, sl in enumerate(lanes)])
    for sl, y in zip(lanes, ys):
        y_ref[:, sl] = y.astype(BF16)


def _rwkv_recur(r, ld, k, v, a, kkp, kap, rkp, gnw, gnb, B, T, G=8):
    N, D = r.shape
    L = RWKV_CHUNK
    W = G * LANES
    nc = T // L
    seq = pl.BlockSpec((L, W), lambda b, p, c: (b * nc + c, p))
    par = pl.BlockSpec((1, W), lambda b, p, c: (0, p))
    return pl.pallas_call(
        functools.partial(_rwkv_recur_kernel, G),
        grid=(B, D // W, nc),
        in_specs=[seq] * 5 + [par] * 5,
        out_specs=seq,
        out_shape=jax.ShapeDtypeStruct((N, D), BF16),
        scratch_shapes=[pltpu.VMEM((G, LANES, LANES), F32)],
        compiler_params=_cparams(("parallel", "parallel", "arbitrary")),
        name="rwkv_recur",
    )(r, ld, k, v, a, kkp, kap, rkp, gnw, gnb)


def _swiglu_chunk(u, w1, w3, w2):
    return _dot(_silu(_dot(u, w1)) * _dot(u, w3), w2)


def _dense_ffn_kernel(h_ref, y_ref, g_ref, wo_ref, nw_ref, w1_ref, w3_ref, w2_ref, o_ref, u_ref, acc_ref):
    f = pl.program_id(1)

    @pl.when(f == 0)
    def _():
        h = h_ref[...] + _dot(y_ref[...].astype(F32) * g_ref[...].astype(F32), wo_ref[...])
        u_ref[...] = _rms(h, nw_ref[...]).astype(BF16)
        acc_ref[...] = h

    acc_ref[...] += _swiglu_chunk(u_ref[...], w1_ref[...], w3_ref[...], w2_ref[...])

    @pl.when(f == pl.num_programs(1) - 1)
    def _():
        o_ref[...] = acc_ref[...]


def _mix_out_dense_ffn(h, y, g, wo, nw, w1, w3, w2, tm=512, tf=1408):
    N, D = h.shape
    Fd = w1.shape[1]
    tm = min(tm, N)
    row = pl.BlockSpec((tm, D), lambda i, f: (i, 0))
    return pl.pallas_call(
        _dense_ffn_kernel,
        grid=(N // tm, Fd // tf),
        in_specs=[row, row, row, _full(wo.shape), _full(nw.shape),
                  pl.BlockSpec((D, tf), lambda i, f: (0, f)), pl.BlockSpec((D, tf), lambda i, f: (0, f)),
                  pl.BlockSpec((tf, D), lambda i, f: (f, 0))],
        out_specs=row,
        out_shape=jax.ShapeDtypeStruct((N, D), F32),
        scratch_shapes=[pltpu.VMEM((tm, D), BF16), pltpu.VMEM((tm, D), F32)],
        compiler_params=_cparams(("parallel", "arbitrary")),
        name="dense_ffn",
    )(h, y, g, wo, nw, w1, w3, w2)


def _ret_proj_kernel(dk, h_ref, nw_ref, cos_ref, sin_ref, wq_ref, wk_ref, wv_ref, wg_ref,
                     q_out, k_out, v_out, g_out):
    u = _rms(h_ref[...], nw_ref[...]).astype(BF16)
    nh = wq_ref.shape[1] // dk
    cos = jnp.concatenate([cos_ref[...]] * nh, axis=1)
    sin = jnp.concatenate([sin_ref[...]] * nh, axis=1)
    lane = lax.broadcasted_iota(jnp.int32, cos.shape, 1)
    even = (lane & 1) == 0
    width = cos.shape[1]

    def rope(t):
        nxt = pltpu.roll(t, width - 1, 1)
        prv = pltpu.roll(t, 1, 1)
        return t * cos + jnp.where(even, nxt, prv) * sin

    q_out[...] = rope(_dot(u, wq_ref[...])).astype(BF16)
    k_out[...] = rope(_dot(u, wk_ref[...]) * (dk ** -0.5)).astype(BF16)
    v_out[...] = _dot(u, wv_ref[...]).astype(BF16)
    g_out[...] = _dot(u, wg_ref[...]).astype(BF16)


def _ret_proj(h, T, nw, cos, sin, wq, wk, wv, wg, tm=256):
    N, D = h.shape
    dk = cos.shape[1]
    tm = min(tm, T)
    tpb = T // tm
    row = lambda w: pl.BlockSpec((tm, w), lambda i: (i, 0))
    tab = pl.BlockSpec((tm, dk), lambda i: (i % tpb, 0))
    return pl.pallas_call(
        functools.partial(_ret_proj_kernel, dk),
        grid=(N // tm,),
        in_specs=[row(D), _full(nw.shape), tab, tab] + [_full(w.shape) for w in (wq, wk, wv, wg)],
        out_specs=[row(wq.shape[1]), row(wk.shape[1]), row(wv.shape[1]), row(wg.shape[1])],
        out_shape=[jax.ShapeDtypeStruct((N, w.shape[1]), BF16) for w in (wq, wk, wv, wg)],
        compiler_params=_cparams(("parallel",)),
        name="ret_proj",
    )(h, nw, cos, sin, wq, wk, wv, wg)


def _ret_head_chunk(q, k, v, g, intra, qd, kd, cd, r_ref):
    r0 = r_ref[...]
    s = _dot_nt(q, k) * intra
    inter = _dot(q, r0) * qd
    r_ref[...] = r0 * cd + _dot_tn(k.astype(F32) * kd, v)
    yield
    o = _dot(s, v) + inter
    yield
    o = o * lax.rsqrt(jnp.mean(o * o, axis=-1, keepdims=True) + RMS_EPS)
    return (_silu(g.astype(F32)) * o).astype(BF16)


def _ret_chunk_kernel(H, q_ref, k_ref, v_ref, g_ref, intra_ref, qd_ref, kd_ref, cd_ref, o_ref, r_ref):
    @pl.when(pl.program_id(1) == 0)
    def _():
        r_ref[...] = jnp.zeros_like(r_ref)

    dk = q_ref.shape[1] // H
    dv = v_ref.shape[1] // H
    ks = [slice(h * dk, (h + 1) * dk) for h in range(H)]
    vs = [slice(h * dv, (h + 1) * dv) for h in range(H)]
    outs = _run_in_lockstep([
        _ret_head_chunk(q_ref[:, ks[h]], k_ref[:, ks[h]], v_ref[:, vs[h]], g_ref[:, vs[h]],
                        intra_ref[h], qd_ref[h], kd_ref[h], cd_ref[h], r_ref.at[h])
        for h in range(H)])
    for h in range(H):
        o_ref[:, vs[h]] = outs[h]


def _ret_chunk(q, k, v, g, B, T, H):
    N = q.shape[0]
    dk = q.shape[1] // H
    dv = v.shape[1] // H
    C = min(RET_BLOCK, T)
    nc = T // C
    lg = jnp.log(1.0 - 2.0 ** (-5.0 - jnp.arange(H, dtype=F32)))
    n = jnp.arange(C, dtype=F32)
    diff = n[:, None] - n[None, :]
    intra = jnp.where(diff >= 0, jnp.exp(lg[:, None, None] * jnp.maximum(diff, 0.0)), 0.0)
    qd = jnp.broadcast_to(jnp.exp(lg[:, None] * (n[None, :] + 1.0))[:, :, None], (H, C, LANES))
    kd = jnp.broadcast_to(jnp.exp(lg[:, None] * (C - 1.0 - n[None, :]))[:, :, None], (H, C, LANES))
    cd = jnp.broadcast_to(jnp.exp(lg * C)[:, None, None], (H, 8, LANES))
    qd = jnp.concatenate([qd] * (dv // LANES), axis=2)
    kd = jnp.concatenate([kd] * (dk // LANES), axis=2)
    cd = jnp.concatenate([cd] * (dv // LANES), axis=2)[:, :1]
    seq = lambda w: pl.BlockSpec((C, w), lambda b, c: (b * nc + c, 0))
    return pl.pallas_call(
        functools.partial(_ret_chunk_kernel, H),
        grid=(B, nc),
        in_specs=[seq(H * dk), seq(H * dk), seq(H * dv), seq(H * dv)]
        + [_full(t.shape) for t in (intra, qd, kd, cd)],
        out_specs=seq(H * dv),
        out_shape=jax.ShapeDtypeStruct((N, H * dv), BF16),
        scratch_shapes=[pltpu.VMEM((H, dk, dv), F32)],
        compiler_params=_cparams(("parallel", "arbitrary")),
        name="ret_chunk",
    )(q, k, v, g, intra, qd, kd, cd)


def _ret_out_router_kernel(h_ref, o_ref, wo_ref, nw_ref, rt_ref, h_out, u_out, idx_out, gate_out):
    h = h_ref[...] + _dot(o_ref[...], wo_ref[...])
    h_out[...] = h
    u = _rms(h, nw_ref[...])
    u_out[...] = u
    logits = lax.dot_general(rt_ref[...], u, (((1,), (1,)), ((), ())),
                             precision=lax.Precision.HIGHEST, preferred_element_type=F32)
    e_iota = lax.broadcasted_iota(jnp.int32, logits.shape, 0)
    big = jnp.int32(N_EXPERTS)
    m1 = jnp.max(logits, axis=0, keepdims=True)
    i1 = jnp.min(jnp.where(logits == m1, e_iota, big), axis=0, keepdims=True)
    rest = jnp.where(e_iota == i1, -jnp.inf, logits)
    m2 = jnp.max(rest, axis=0, keepdims=True)
    i2 = jnp.min(jnp.where(rest == m2, e_iota, big), axis=0, keepdims=True)
    p2 = jnp.exp(m2 - m1)
    g1 = 1.0 / (1.0 + p2)
    idx_out[...] = jnp.concatenate([i1, i2], axis=0)
    gate_out[...] = jnp.concatenate([g1, p2 * g1], axis=0)


def _ret_out_router(h, o, wo, nw, router_t, tm=512):
    N, D = h.shape
    K = o.shape[1]
    tm = min(tm, N)
    row = lambda w: pl.BlockSpec((tm, w), lambda i: (i, 0))
    col = pl.BlockSpec((TOP_K, tm), lambda i: (0, i))
    return pl.pallas_call(
        _ret_out_router_kernel,
        grid=(N // tm,),
        in_specs=[row(D), row(K), _full(wo.shape), _full(nw.shape), _full(router_t.shape)],
        out_specs=[row(D), row(D), col, col],
        out_shape=[jax.ShapeDtypeStruct((N, D), F32), jax.ShapeDtypeStruct((N, D), F32),
                   jax.ShapeDtypeStruct((TOP_K, N), jnp.int32), jax.ShapeDtypeStruct((TOP_K, N), F32)],
        compiler_params=_cparams(("parallel",)),
        name="ret_out_router",
    )(h, o, wo, nw, router_t)


def _row_copy(src_hbm, dst_ref, sem, s, j):
    return pltpu.make_async_copy(src_hbm.at[pl.ds(s, 1), :], dst_ref.at[pl.ds(j, 1), :], sem)


def _gather_rows_kernel(tg, src_ref, u_hbm, o_ref, sem):
    def issue(j, c):
        _row_copy(u_hbm, o_ref, sem, src_ref[j], j).start()
        return c

    lax.fori_loop(0, tg, issue, 0, unroll=DMA_UNROLL)

    def drain(j, c):
        _row_copy(u_hbm, o_ref, sem, 0, j).wait()
        return c

    lax.fori_loop(0, tg, drain, 0, unroll=DMA_UNROLL)


def _gather_rows(u, src, tg=512):
    R = src.shape[0]
    D = u.shape[1]
    return pl.pallas_call(
        functools.partial(_gather_rows_kernel, tg),
        grid=(R // tg,),
        in_specs=[pl.BlockSpec((tg,), lambda i: (i,), memory_space=pltpu.SMEM),
                  pl.BlockSpec(memory_space=pl.ANY)],
        out_specs=pl.BlockSpec((tg, D), lambda i: (i, 0)),
        out_shape=jax.ShapeDtypeStruct((R, D), u.dtype),
        scratch_shapes=[pltpu.SemaphoreType.DMA(())],
        compiler_params=_cparams(("arbitrary",)),
        name="moe_gather",
    )(src, u)


def _expert_kernel(te_ref, nv_ref, x_ref, w1_ref, w3_ref, w2_ref, o_ref, xb_ref, acc_ref):
    i = pl.program_id(0)
    f = pl.program_id(1)
    valid = i < nv_ref[0]

    @pl.when(valid)
    def _():
        @pl.when(f == 0)
        def _():
            xb_ref[...] = x_ref[...].astype(BF16)
            acc_ref[...] = jnp.zeros_like(acc_ref)

        acc_ref[...] += _swiglu_chunk(xb_ref[...], w1_ref[0], w3_ref[0], w2_ref[0])

    @pl.when(f == pl.num_programs(1) - 1)
    def _():
        o_ref[...] = jnp.where(valid, acc_ref[...], 0.0)


def _experts(xs, tile_expert, n_valid, w1, w3, w2, tf=896):
    R, D = xs.shape
    Fd = w1.shape[2]
    tmm = MOE_TILE
    nf = Fd // tf

    def w13_map(i, f, te, nv):
        return (te[i], 0, jnp.where(i < nv[0], f, nf - 1))

    def w2_map(i, f, te, nv):
        return (te[i], jnp.where(i < nv[0], f, nf - 1), 0)

    gs = pltpu.PrefetchScalarGridSpec(
        num_scalar_prefetch=2,
        grid=(R // tmm, nf),
        in_specs=[pl.BlockSpec((tmm, D), lambda i, f, te, nv: (i, 0)),
                  pl.BlockSpec((1, D, tf), w13_map), pl.BlockSpec((1, D, tf), w13_map),
                  pl.BlockSpec((1, tf, D), w2_map)],
        out_specs=pl.BlockSpec((tmm, D), lambda i, f, te, nv: (i, 0)),
        scratch_shapes=[pltpu.VMEM((tmm, D), BF16), pltpu.VMEM((tmm, D), F32)],
    )
    return pl.pallas_call(
        _expert_kernel,
        grid_spec=gs,
        out_shape=jax.ShapeDtypeStruct((R, D), F32),
        compiler_params=_cparams(("arbitrary", "arbitrary")),
        name="moe_experts",
    )(tile_expert, n_valid, xs, w1, w3, w2)


def _combine_kernel(tc, pos_ref, h_ref, gate_ref, nw_ref, ys_hbm, o_ref, buf_ref, sem):
    def issue(j, c):
        _row_copy(ys_hbm, buf_ref.at[0], sem, pos_ref[0, j], j).start()
        _row_copy(ys_hbm, buf_ref.at[1], sem, pos_ref[1, j], j).start()
        return c

    lax.fori_loop(0, tc, issue, 0, unroll=DMA_UNROLL)

    def drain(j, c):
        _row_copy(ys_hbm, buf_ref.at[0], sem, 0, j).wait()
        _row_copy(ys_hbm, buf_ref.at[1], sem, 0, j).wait()
        return c

    lax.fori_loop(0, tc, drain, 0, unroll=DMA_UNROLL)
    gate = gate_ref[...]
    moe = gate[:, 0:1] * buf_ref[0] + gate[:, 1:2] * buf_ref[1]
    o_ref[...] = _rms(h_ref[...] + moe, nw_ref[...])


def _combine(h, ys, pos, gate, nw, tc=256):
    N, D = h.shape
    tc = min(tc, N)
    return pl.pallas_call(
        functools.partial(_combine_kernel, tc),
        grid=(N // tc,),
        in_specs=[pl.BlockSpec((TOP_K, tc), lambda i: (0, i), memory_space=pltpu.SMEM),
                  pl.BlockSpec((tc, D), lambda i: (i, 0)),
                  pl.BlockSpec((tc, TOP_K), lambda i: (i, 0)),
                  _full(nw.shape),
                  pl.BlockSpec(memory_space=pl.ANY)],
        out_specs=pl.BlockSpec((tc, D), lambda i: (i, 0)),
        out_shape=jax.ShapeDtypeStruct((N, D), F32),
        scratch_shapes=[pltpu.VMEM((TOP_K, tc, D), F32), pltpu.SemaphoreType.DMA(())],
        compiler_params=_cparams(("arbitrary",)),
        name="moe_combine",
    )(pos, h, gate, nw, ys)


def _route_plan(idx, N):
    tmm = MOE_TILE
    n_tiles = (TOP_K * N) // tmm + N_EXPERTS
    flat_e = idx.T.reshape(-1)
    onehot = (flat_e[:, None] == jnp.arange(N_EXPERTS, dtype=jnp.int32)[None, :]).astype(jnp.int32)
    csum = jnp.cumsum(onehot, axis=0)
    rank = jnp.sum(csum * onehot, axis=1) - 1
    counts = csum[-1]
    ptiles = (counts + tmm - 1) // tmm
    tile_end = jnp.cumsum(ptiles)
    tile_start = tile_end - ptiles
    pos = tile_start[flat_e] * tmm + rank
    token = jnp.arange(TOP_K * N, dtype=jnp.int32) // TOP_K
    src = jnp.zeros((n_tiles * tmm,), jnp.int32).at[pos].set(token)
    n_valid = tile_end[-1]
    t = jnp.minimum(jnp.arange(n_tiles, dtype=jnp.int32), n_valid - 1)
    tile_expert = jnp.sum((t[:, None] >= tile_end[None, :]).astype(jnp.int32), axis=1)
    return src, pos.reshape(N, TOP_K).T.astype(jnp.int32), tile_expert.astype(jnp.int32), \
        n_valid.reshape(1).astype(jnp.int32)


def _rope_tables(T, dk):
    angle = 1.0 / (ROPE_BASE ** jnp.linspace(0.0, 1.0, dk // 2, dtype=F32))
    angle = jnp.repeat(angle, 2)
    ang = jnp.arange(T, dtype=F32)[:, None] * angle[None, :]
    sign = jnp.where(jnp.arange(dk) % 2 == 0, -1.0, 1.0).astype(F32)
    return jnp.cos(ang), jnp.sin(ang) * sign[None, :]


def kernel(x, norm_mix, norm_ffn, norm_final, a_mu, a_wr, a_wk, a_wv, a_wo, a_w0, a_w1, a_w2, a_a0, a_a1, a_a2, a_g1, a_g2, a_kk, a_ka, a_rk, a_gn_w, a_gn_b, b_wq, b_wk, b_wv, b_wg, b_wo, f_w1, f_w3, f_w2, m_router, m_w1, m_w3, m_w2):
    B, T, D = x.shape
    N = B * T
    bf = lambda w: w.astype(BF16)
    vec = lambda w: w.reshape(1, -1).astype(F32)
    h = x.reshape(N, D)

    r, ld, k, v, a, g = _rwkv_proj(
        h, T, vec(norm_mix[0]), a_mu[0], bf(a_wr[0]), bf(a_wk[0]), bf(a_wv[0]), bf(a_w1[0]), bf(a_w2[0]),
        bf(a_a1[0]), bf(a_a2[0]), bf(a_g1[0]), bf(a_g2[0]), vec(a_w0[0]), vec(a_a0[0]))
    y = _rwkv_recur(r, ld, k, v, a, vec(a_kk[0]), vec(a_ka[0]), vec(a_rk[0]), vec(a_gn_w[0]),
                    vec(a_gn_b[0]), B, T)
    h = _mix_out_dense_ffn(h, y, g, bf(a_wo[0]), vec(norm_ffn[0]), bf(f_w1[0]), bf(f_w3[0]), bf(f_w2[0]))

    dk = b_wq.shape[2] // RET_HEADS
    cos, sin = _rope_tables(T, dk)
    q, kr, vr, gr = _ret_proj(h, T, vec(norm_mix[1]), cos, sin, bf(b_wq[0]), bf(b_wk[0]), bf(b_wv[0]),
                              bf(b_wg[0]))
    o = _ret_chunk(q, kr, vr, gr, B, T, RET_HEADS)
    h, u, idx, gate = _ret_out_router(h, o, bf(b_wo[0]), vec(norm_ffn[1]), m_router[0].T.astype(F32))
    src, pos, tile_expert, n_valid = _route_plan(idx, N)
    xs = _gather_rows(u, src)
    ys = _experts(xs, tile_expert, n_valid, bf(m_w1[0]), bf(m_w3[0]), bf(m_w2[0]))
    out = _combine(h, ys, pos, gate.T, vec(norm_final))
    return out.reshape(B, T, D)
```

```python
import functools
import math

import jax
import jax.numpy as jnp
from jax import lax
from jax.experimental import pallas as pl
from jax.experimental.pallas import tpu as pltpu

F32 = jnp.float32
BF16 = jnp.bfloat16

RMS_EPS = 1e-6
RWKV_HEAD = 64
RWKV_GN_EPS = 64e-5
RET_HEADS = 4
ROPE_BASE = 10000.0
N_EXPERTS = 8
TOP_K = 2

LANES = 128
VMEM_LIMIT = 56 * 1024 * 1024

RWKV_CHUNK = 64
RWKV_SUB = 16
RET_BLOCK = 256
MOE_TILE = 512
MOE_SUB = 256
DMA_UNROLL = 8


def _cparams(sem):
    return pltpu.CompilerParams(dimension_semantics=sem, vmem_limit_bytes=VMEM_LIMIT)


def _dot(a, b):
    return jnp.dot(a.astype(BF16), b.astype(BF16), preferred_element_type=F32)


def _dot_nt(a, b):
    return lax.dot_general(a.astype(BF16), b.astype(BF16), (((1,), (1,)), ((), ())),
                           preferred_element_type=F32)


def _dot_tn(a, b):
    return lax.dot_general(a.astype(BF16), b.astype(BF16), (((0,), (0,)), ((), ())),
                           preferred_element_type=F32)


def _rms(x, g):
    return x * lax.rsqrt(jnp.mean(x * x, axis=-1, keepdims=True) + RMS_EPS) * g


def _sigmoid(x):
    return 1.0 / (1.0 + jnp.exp(-x))


def _silu(x):
    return x * _sigmoid(x)


def _full(shape):
    n = len(shape)
    return pl.BlockSpec(shape, lambda *_: (0,) * n)


def _rwkv_proj_kernel(T, tm, h_ref, hp_ref, nw_ref, mu_ref, wr_ref, wk_ref, wv_ref, w1_ref, w2_ref,
                      a1_ref, a2_ref, g1_ref, g2_ref, w0_ref, a0_ref,
                      r_out, ld_out, k_out, v_out, a_out, g_out):
    i = pl.program_id(0)
    nw = nw_ref[...]
    u = _rms(h_ref[...], nw)
    up = _rms(hp_ref[...], nw)
    first = (i * tm) % T == 0
    prev_last = jnp.where(first, 0.0, up[7:8, :])
    row = lax.broadcasted_iota(jnp.int32, u.shape, 0)
    us = jnp.where(row == 0, prev_last, pltpu.roll(u, 1, 0))
    dx = us - u

    def mix(j):
        return (u + dx * mu_ref[j:j + 1, :]).astype(BF16)

    r_out[...] = _dot(mix(0), wr_ref[...]).astype(BF16)
    z =w0_ref[...] + _dot(jnp.tanh(_dot(mix(1), w1_ref[...])), w2_ref[...])
    nz = -z
    softplus = jnp.maximum(nz, 0.0) + jnp.log(1.0 + jnp.exp(-jnp.abs(nz)))
    ld_out[...] = -jnp.exp(-softplus - 0.5)
    k_out[...] = _dot(mix(2), wk_ref[...]).astype(BF16)
    v_out[...] = _dot(mix(3), wv_ref[...]).astype(BF16)
    a_out[...] = _sigmoid(a0_ref[...] + _dot(_dot(mix(4), a1_ref[...]), a2_ref[...])).astype(BF16)
    g_out[...] = _dot(_sigmoid(_dot(mix(5), g1_ref[...])), g2_ref[...]).astype(BF16)


def _rwkv_proj(h, T, nw, mu, wr, wk, wv, w1, w2, a1, a2, g1, g2, w0, a0, tm=256):
    N, D = h.shape
    tm = min(tm, T)
    row = pl.BlockSpec((tm, D), lambda i: (i, 0))
    prev = pl.BlockSpec((8, D), lambda i: (jnp.maximum(i * (tm // 8) - 1, 0), 0))
    ws = [nw, mu, wr, wk, wv, w1, w2, a1, a2, g1, g2, w0, a0]
    out = lambda dt: jax.ShapeDtypeStruct((N, D), dt)
    return pl.pallas_call(
        functools.partial(_rwkv_proj_kernel, T, tm),
        grid=(N // tm,),
        in_specs=[row, prev] + [_full(w.shape) for w in ws],
        out_specs=[row] * 6,
        out_shape=[out(BF16), out(F32), out(BF16), out(BF16), out(BF16), out(BF16)],
        compiler_params=_cparams(("parallel",)),
        name="rwkv_proj",
    )(h, h, *ws)


def _rwkv_pair_chunk(r, ld, k, v, a, kkp, kap, rkp, gnw, gnb, s_ref):
    L = r.shape[0]
    L2 = 2 * L
    lane = lax.broadcasted_iota(jnp.int32, (L, LANES), 1)
    m0 = lane < RWKV_HEAD

    def seg_sum(x):
        s0 = jnp.sum(jnp.where(m0, x, 0.0), axis=-1, keepdims=True)
        s1 = jnp.sum(jnp.where(m0, 0.0, x), axis=-1, keepdims=True)
        return jnp.where(m0, s0, s1)

    def stack(x):
        return jnp.concatenate([jnp.where(m0, x, 0.0), jnp.where(m0, 0.0, x)], axis=0)

    kk = k * kkp
    kk = kk / jnp.maximum(jnp.sqrt(seg_sum(kk * kk)), 1e-12)
    kmod = k * (1.0 + (a - 1.0) * kap)
    aa = -kk
    bb = kk * a

    ti = lax.broadcasted_iota(jnp.int32, (L, L), 0)
    tj = lax.broadcasted_iota(jnp.int32, (L, L), 1)
    tri = jnp.where(tj <= ti, 1.0, 0.0).astype(F32)
    cum = jnp.dot(tri, ld, precision=lax.Precision.HIGHEST, preferred_element_type=F32)
    yield
    tot = cum[L - 1:L, :]
    e_cum = jnp.exp(cum)
    e_inv = jnp.exp(-cum)
    e_fin = jnp.exp(tot - cum)
    a_t = stack(aa * jnp.exp(cum - ld))
    r_t = stack(r * e_cum)
    b_t = stack(bb * e_inv)
    k_t = stack(kmod * e_inv)
    b_h = stack(bb * e_fin)
    k_h = stack(kmod * e_fin)
    v_s = stack(v)

    mm = _dot_nt(jnp.concatenate([a_t, r_t], axis=0), jnp.concatenate([b_t, k_t], axis=0))
    yield
    si =lax.broadcasted_iota(jnp.int32, (L2, L2), 0)
    sj = lax.broadcasted_iota(jnp.int32, (L2, L2), 1)
    low = sj < si
    lowi = sj <= si
    m_ab = jnp.where(low, mm[:L2, :L2], 0.0)
    m_ak = jnp.where(low, mm[:L2, L2:], 0.0)
    m_rb = jnp.where(lowi, mm[L2:, :L2], 0.0)
    m_rk = jnp.where(lowi, mm[L2:, L2:], 0.0)

    eye = jnp.where(si == sj, 1.0, 0.0).astype(F32)
    sub_shift = RWKV_SUB.bit_length() - 1
    same = (si >> sub_shift) == (sj >> sub_shift)
    dg = jnp.where(same, m_ab, 0.0)
    og = jnp.where(same, 0.0, m_ab)
    d2 = _dot(dg, dg)
    s0 = s_ref[...]
    c = _dot_nt(a_t, s0) + _dot(m_ak, v_s)
    yield
    d4 = _dot(d2, d2)
    x1 = eye + dg + d2 + _dot(dg, d2)
    yield
    d8 = _dot(d4, d4)
    yield
    x2 = eye + d4 + d8 + _dot(d4, d8)
    yield
    t_d = _dot(x1, x2)
    yield
    e1 = _dot(t_d, og)
    yield
    e2 = _dot(e1, e1)
    yield
    t_f = eye + e1 + e2 + _dot(e1, e2)
    yield
    t_m = _dot(t_f, t_d)
    yield
    u = _dot(t_m, c)
    yield
    ys = _dot_nt(r_t, s0) + _dot(m_rb, u) + _dot(m_rk, v_s)
    s_ref[...] = s0 * jnp.exp(tot) + _dot_tn(jnp.concatenate([u, v_s], axis=0),
                                            jnp.concatenate([b_h, k_h], axis=0))
    yield
    y = ys[:L] + ys[L:]

    inv_n = 1.0 / RWKV_HEAD
    mean = seg_sum(y) * inv_n
    yc = y - mean
    var = seg_sum(yc * yc) * inv_n
    yn = yc * lax.rsqrt(var + RWKV_GN_EPS) * gnw + gnb
    return yn + seg_sum(r * kmod * rkp) * v


def _run_in_lockstep(gens):
    results = [None] * len(gens)
    live = list(range(len(gens)))
    while live:
        still = []
        for i in live:
            try:
                next(gens[i])
                still.append(i)
            except StopIteration as done:
                results[i] = done.value
        live = still
    return results


def _rwkv_recur_kernel(G, r_ref, ld_ref, k_ref, v_ref, a_ref, kk_ref, ka_ref, rk_ref, gw_ref, gb_ref,
                       y_ref, s_ref):
    @pl.when(pl.program_id(2) == 0)
    def _():
        s_ref[...] = jnp.zeros_like(s_ref)

    lanes = [slice(g * LANES, (g + 1) * LANES) for g in range(G)]
    f32 = lambda ref, sl: ref[:, sl].astype(F32)
    ys = _run_in_lockstep([
        _rwkv_pair_chunk(f32(r_ref, sl), ld_ref[:, sl], f32(k_ref, sl), f32(v_ref, sl), f32(a_ref, sl),
                         kk_ref[:, sl], ka_ref[:, sl], rk_ref[:, sl], gw_ref[:, sl], gb_ref[:, sl],
                         s_ref.at[g])
        for g, sl in enumerate(lanes)])
    for sl, y in zip(lanes, ys):
        y_ref[:, sl] = y.astype(BF16)


def _rwkv_recur(r, ld, k, v, a, kkp, kap, rkp, gnw, gnb, B, T, G=8):
    N, D = r.shape
    L = RWKV_CHUNK
    W = G * LANES
    nc = T // L
    seq = pl.BlockSpec((L, W), lambda b, p, c: (b * nc + c, p))
    par = pl.BlockSpec((1, W), lambda b, p, c: (0, p))
    return pl.pallas_call(
        functools.partial(_rwkv_recur_kernel, G),
        grid=(B, D // W, nc),
        in_specs=[seq] * 5 + [par] * 5,
        out_specs=seq,
        out_shape=jax.ShapeDtypeStruct((N, D), BF16),
        scratch_shapes=[pltpu.VMEM((G, LANES, LANES), F32)],
        compiler_params=_cparams(("parallel", "parallel", "arbitrary")),
        name="rwkv_recur",
    )(r, ld, k, v, a, kkp, kap, rkp, gnw, gnb)


def _swiglu_chunk(u, w1, w3, w2):
    return _dot(_silu(_dot(u, w1)) * _dot(u, w3), w2)


def _dense_ffn_kernel(h_ref, y_ref, g_ref, wo_ref, nw_ref, w1_ref, w3_ref, w2_ref, o_ref, u_ref, acc_ref):
    f = pl.program_id(1)

    @pl.when(f == 0)
    def _():
        h = h_ref[...] + _dot(y_ref[...].astype(F32) * g_ref[...].astype(F32), wo_ref[...])
        u_ref[...] = _rms(h, nw_ref[...]).astype(BF16)
        acc_ref[...] = h

    acc_ref[...] += _swiglu_chunk(u_ref[...], w1_ref[...], w3_ref[...], w2_ref[...])

    @pl.when(f == pl.num_programs(1) - 1)
    def _():
        o_ref[...] = acc_ref[...]


def _mix_out_dense_ffn(h, y, g, wo, nw, w1, w3, w2, tm=512, tf=1408):
    N, D = h.shape
    Fd = w1.shape[1]
    tm = min(tm, N)
    row = pl.BlockSpec((tm, D), lambda i, f: (i, 0))
    return pl.pallas_call(
        _dense_ffn_kernel,
        grid=(N // tm, Fd // tf),
        in_specs=[row, row, row, _full(wo.shape), _full(nw.shape),
                  pl.BlockSpec((D, tf), lambda i, f: (0, f)), pl.BlockSpec((D, tf), lambda i, f: (0, f)),
                  pl.BlockSpec((tf, D), lambda i, f: (f, 0))],
        out_specs=row,
        out_shape=jax.ShapeDtypeStruct((N, D), F32),
        scratch_shapes=[pltpu.VMEM((tm, D), BF16), pltpu.VMEM((tm, D), F32)],
        compiler_params=_cparams(("parallel", "arbitrary")),
        name="dense_ffn",
    )(h, y, g, wo, nw, w1, w3, w2)


def _ret_proj_kernel(dk, h_ref, nw_ref, cos_ref, sin_ref, wq_ref, wk_ref, wv_ref, wg_ref,
                     q_out, k_out, v_out, g_out):
    u = _rms(h_ref[...], nw_ref[...]).astype(BF16)
    nh = wq_ref.shape[1] // dk
    cos = jnp.concatenate([cos_ref[...]] * nh, axis=1)
    sin = jnp.concatenate([sin_ref[...]] * nh, axis=1)
    lane = lax.broadcasted_iota(jnp.int32, cos.shape, 1)
    even = (lane & 1) == 0
    width = cos.shape[1]

    def rope(t):
        nxt = pltpu.roll(t, width - 1, 1)
        prv = pltpu.roll(t, 1, 1)
        return t * cos + jnp.where(even, nxt, prv) * sin

    q_out[...] = rope(_dot(u, wq_ref[...])).astype(BF16)
    k_out[...] = rope(_dot(u, wk_ref[...]) * (dk ** -0.5)).astype(BF16)
    v_out[...] = _dot(u, wv_ref[...]).astype(BF16)
    g_out[...] = _dot(u, wg_ref[...]).astype(BF16)


def _ret_proj(h, T, nw, cos, sin, wq, wk, wv, wg, tm=256):
    N, D = h.shape
    dk = cos.shape[1]
    tm = min(tm, T)
    tpb = T // tm
    row = lambda w: pl.BlockSpec((tm, w), lambda i: (i, 0))
    tab = pl.BlockSpec((tm, dk), lambda i: (i % tpb, 0))
    return pl.pallas_call(
        functools.partial(_ret_proj_kernel, dk),
        grid=(N // tm,),
        in_specs=[row(D), _full(nw.shape), tab, tab] + [_full(w.shape) for w in (wq, wk, wv, wg)],
        out_specs=[row(wq.shape[1]), row(wk.shape[1]), row(wv.shape[1]), row(wg.shape[1])],
        out_shape=[jax.ShapeDtypeStruct((N, w.shape[1]), BF16) for w in (wq, wk, wv, wg)],
        compiler_params=_cparams(("parallel",)),
        name="ret_proj",
    )(h, nw, cos, sin, wq, wk, wv, wg)


def _ret_head_chunk(q, k, v, g, intra, qd, kd, cd, r_ref):
    r0 = r_ref[...]
    s = _dot_nt(q, k) * intra
    inter = _dot(q, r0) * qd
    r_ref[...] = r0 * cd + _dot_tn(k.astype(F32) * kd, v)
    yield
    o = _dot(s, v) + inter
    yield
    o = o * lax.rsqrt(jnp.mean(o * o, axis=-1, keepdims=True) + RMS_EPS)
    return (_silu(g.astype(F32)) * o).astype(BF16)


def _ret_chunk_kernel(H, q_ref, k_ref, v_ref, g_ref, intra_ref, qd_ref, kd_ref, cd_ref, o_ref, r_ref):
    @pl.when(pl.program_id(1) == 0)
    def _():
        r_ref[...] = jnp.zeros_like(r_ref)

    dk = q_ref.shape[1] // H
    dv = v_ref.shape[1] // H
    ks = [slice(h * dk, (h + 1) * dk) for h in range(H)]
    vs = [slice(h * dv, (h + 1) * dv) for h in range(H)]
    outs = _run_in_lockstep([
        _ret_head_chunk(q_ref[:, ks[h]], k_ref[:, ks[h]], v_ref[:, vs[h]], g_ref[:, vs[h]],
                        intra_ref[h], qd_ref[h], kd_ref[h], cd_ref[h], r_ref.at[h])
        for h in range(H)])
    for h in range(H):
        o_ref[:, vs[h]] = outs[h]


def _ret_chunk(q, k, v, g, B, T, H):
    N = q.shape[0]
    dk = q.shape[1] // H
    dv = v.shape[1] // H
    C = min(RET_BLOCK, T)
    nc = T // C
    lg = jnp.log(1.0 - 2.0 ** (-5.0 - jnp.arange(H, dtype=F32)))
    n = jnp.arange(C, dtype=F32)
    diff = n[:, None] - n[None, :]
    intra = jnp.where(diff >= 0, jnp.exp(lg[:, None, None] * jnp.maximum(diff, 0.0)), 0.0)
    qd = jnp.broadcast_to(jnp.exp(lg[:, None] * (n[None, :] + 1.0))[:, :, None], (H, C, LANES))
    kd = jnp.broadcast_to(jnp.exp(lg[:, None] * (C - 1.0 - n[None, :]))[:, :, None], (H, C, LANES))
    cd = jnp.broadcast_to(jnp.exp(lg * C)[:, None, None], (H, 8, LANES))
    qd = jnp.concatenate([qd] * (dv // LANES), axis=2)
    kd = jnp.concatenate([kd] * (dk // LANES), axis=2)
    cd = jnp.concatenate([cd] * (dv // LANES), axis=2)[:, :1]
    seq = lambda w: pl.BlockSpec((C, w), lambda b, c: (b * nc + c, 0))
    return pl.pallas_call(
        functools.partial(_ret_chunk_kernel, H),
        grid=(B, nc),
        in_specs=[seq(H * dk), seq(H * dk), seq(H * dv), seq(H * dv)]
        + [_full(t.shape) for t in (intra, qd, kd, cd)],
        out_specs=seq(H * dv),
        out_shape=jax.ShapeDtypeStruct((N, H * dv), BF16),
        scratch_shapes=[pltpu.VMEM((H, dk, dv), F32)],
        compiler_params=_cparams(("parallel", "arbitrary")),
        name="ret_chunk",
    )(q, k, v, g, intra, qd, kd, cd)


def _ret_out_router_kernel(h_ref, o_ref, wo_ref, nw_ref, rt_ref, h_out, u_out, idx_out, gate_out):
    h = h_ref[...] + _dot(o_ref[...], wo_ref[...])
    h_out[...] = h
    u = _rms(h, nw_ref[...])
    u_out[...] = u
    logits = lax.dot_general(rt_ref[...], u, (((1,), (1,)), ((), ())),
                             precision=lax.Precision.HIGHEST, preferred_element_type=F32)
    e_iota = lax.broadcasted_iota(jnp.int32, logits.shape, 0)
    big = jnp.int32(N_EXPERTS)
    m1 = jnp.max(logits, axis=0, keepdims=True)
    i1 = jnp.min(jnp.where(logits == m1, e_iota, big), axis=0, keepdims=True)
    rest = jnp.where(e_iota == i1, -jnp.inf, logits)
    m2 = jnp.max(rest, axis=0, keepdims=True)
    i2 = jnp.min(jnp.where(rest == m2, e_iota, big), axis=0, keepdims=True)
    p2 = jnp.exp(m2 - m1)
    g1 = 1.0 / (1.0 + p2)
    idx_out[...] = jnp.concatenate([i1, i2], axis=0)
    gate_out[...] = jnp.concatenate([g1, p2 * g1], axis=0)


def _ret_out_router(h, o, wo, nw, router_t, tm=512):
    N, D = h.shape
    K = o.shape[1]
    tm = min(tm, N)
    row = lambda w: pl.BlockSpec((tm, w), lambda i: (i, 0))
    col = pl.BlockSpec((TOP_K, tm), lambda i: (0, i))
    return pl.pallas_call(
        _ret_out_router_kernel,
        grid=(N // tm,),
        in_specs=[row(D), row(K), _full(wo.shape), _full(nw.shape), _full(router_t.shape)],
        out_specs=[row(D), row(D), col, col],
        out_shape=[jax.ShapeDtypeStruct((N, D), F32), jax.ShapeDtypeStruct((N, D), F32),
                   jax.ShapeDtypeStruct((TOP_K, N), jnp.int32), jax.ShapeDtypeStruct((TOP_K, N), F32)],
        compiler_params=_cparams(("parallel",)),
        name="ret_out_router",
    )(h, o, wo, nw, router_t)


def _row_copy(src_hbm, dst_ref, sem, s, j):
    return pltpu.make_async_copy(src_hbm.at[pl.ds(s, 1), :], dst_ref.at[pl.ds(j, 1), :], sem)


def _moe_kernel(tmm, q, te_ref, src_ref, dst_ref, u_hbm, w1_ref, w3_ref, w2_ref, y_hbm,
                xbuf, xb_ref, acc_ref, obuf, gsem, ssem):
    i = pl.program_id(0)
    f = pl.program_id(1)
    n_i = pl.num_programs(0)
    n_f = pl.num_programs(1)
    cur = i % 2
    oth = 1 - cur

    def gather(slot, j, tok):
        return _row_copy(u_hbm, xbuf.at[slot], gsem.at[slot], tok, j)

    def scatter(slot, j, row):
        return pltpu.make_async_copy(obuf.at[slot].at[pl.ds(j, 1), :], y_hbm.at[pl.ds(row, 1), :],
                                     ssem.at[slot])

    def wait_rows(make):
        def body(j, c):
            make(j).wait()
            return c
        lax.fori_loop(0, tmm, body, 0, unroll=DMA_UNROLL)

    @pl.when((i == 0) & (f == 0))
    def _():
        obuf[1] = jnp.zeros((tmm, obuf.shape[2]), F32)

        def first(j, c):
            gather(0, j, te_ref[n_i + j]).start()
            return c
        lax.fori_loop(0, tmm, first, 0, unroll=DMA_UNROLL)

    @pl.when(f == 0)
    def _():
        wait_rows(lambda j: gather(cur, j, 0))
        xb_ref[...] = xbuf[cur].astype(BF16)
        acc_ref[...] = jnp.zeros_like(acc_ref)

    base = f * q
    n_sub = w1_ref.shape[2] // MOE_SUB
    per = -(-q // n_sub)
    for c in range(n_sub):
        for jj in range(c * per, min((c + 1) * per, q)):
            j = base + jj
            gather(oth, j, src_ref[j]).start()
            scatter(oth, j, dst_ref[j]).start()
        cs = slice(c * MOE_SUB, (c + 1) * MOE_SUB)
        acc_ref[...] += _swiglu_chunk(xb_ref[...], w1_ref[0, :, cs], w3_ref[0, :, cs], w2_ref[0, cs, :])

    @pl.when(f == n_f - 1)
    def _():
        @pl.when(i > 0)
        def _():
            wait_rows(lambda j: scatter(cur, j, 0))
        obuf[cur] = acc_ref[...]

        @pl.when(i == n_i - 1)
        def _():
            def last(j, c):
                scatter(cur, j, te_ref[n_i + tmm + j]).start()
                return c
            lax.fori_loop(0, tmm, last, 0, unroll=DMA_UNROLL)
            wait_rows(lambda j: scatter(oth, j, 0))
            wait_rows(lambda j: scatter(cur, j, 0))
            wait_rows(lambda j: gather(oth, j, 0))


def _moe_experts(u, plan, src_next, dst_prev, w1, w3, w2, n_rows_out, tf=1792):
    N, D = u.shape
    Fd = w1.shape[2]
    tmm = MOE_TILE
    n_tiles = src_next.shape[0] // tmm
    nf = Fd // tf
    q = tmm // nf
    smem = lambda: pl.BlockSpec((tmm,), lambda i, f, te: (i,), memory_space=pltpu.SMEM)
    gs = pltpu.PrefetchScalarGridSpec(
        num_scalar_prefetch=1,
        grid=(n_tiles, nf),
        in_specs=[smem(), smem(), pl.BlockSpec(memory_space=pl.ANY),
                  pl.BlockSpec((1, D, tf), lambda i, f, te: (te[i], 0, f)),
                  pl.BlockSpec((1, D, tf), lambda i, f, te: (te[i], 0, f)),
                  pl.BlockSpec((1, tf, D), lambda i, f, te: (te[i], f, 0))],
        out_specs=pl.BlockSpec(memory_space=pl.ANY),
        scratch_shapes=[pltpu.VMEM((2, tmm, D), F32), pltpu.VMEM((tmm, D), BF16), pltpu.VMEM((tmm, D), F32),
                        pltpu.VMEM((2, tmm, D), F32), pltpu.SemaphoreType.DMA((2,)),
                        pltpu.SemaphoreType.DMA((2,))],
    )
    return pl.pallas_call(
        functools.partial(_moe_kernel, tmm, q),
        grid_spec=gs,
        out_shape=jax.ShapeDtypeStruct((n_rows_out, D), F32),
        compiler_params=_cparams(("arbitrary", "arbitrary")),
        name="moe_experts",
    )(plan, src_next, dst_prev, u, w1, w3, w2)


def _combine_kernel(h_ref, y_ref, gate_ref, nw_ref, o_ref):
    D = h_ref.shape[1]
    gate = gate_ref[...]
    moe = gate[:, 0:1] * y_ref[:, :D] + gate[:, 1:2] * y_ref[:, D:]
    o_ref[...] = _rms(h_ref[...] + moe, nw_ref[...])


def _combine(h, y2, gate, nw, tc=512):
    N, D = h.shape
    tc = min(tc, N)
    return pl.pallas_call(
        _combine_kernel,
        grid=(N // tc,),
        in_specs=[pl.BlockSpec((tc, D), lambda i: (i, 0)),
                  pl.BlockSpec((tc, TOP_K * D), lambda i: (i, 0)),
                  pl.BlockSpec((tc, TOP_K), lambda i: (i, 0)),
                  _full(nw.shape)],
        out_specs=pl.BlockSpec((tc, D), lambda i: (i, 0)),
        out_shape=jax.ShapeDtypeStruct((N, D), F32),
        compiler_params=_cparams(("parallel",)),
        name="moe_combine",
    )(h, y2, gate, nw)


def _route_plan(idx, N):
    tmm = MOE_TILE
    n_asg = TOP_K * N
    n_tiles = n_asg // tmm + N_EXPERTS
    R = n_tiles * tmm
    flat_e = idx.T.reshape(-1)
    onehot = (flat_e[:, None] == jnp.arange(N_EXPERTS, dtype=jnp.int32)[None, :]).astype(jnp.int32)
    csum = jnp.cumsum(onehot, axis=0)
    rank = jnp.sum(csum * onehot, axis=1) - 1
    counts = csum[-1]
    ptiles = (counts + tmm - 1) // tmm
    tile_end = jnp.cumsum(ptiles)
    tile_start = tile_end - ptiles
    pos = tile_start[flat_e] * tmm + rank
    t = jnp.arange(n_tiles, dtype=jnp.int32)
    tile_e = jnp.sum((t[:, None] >= tile_end[None, :]).astype(jnp.int32), axis=1)
    tile_expert = jnp.minimum(tile_e, N_EXPERTS - 1)
    real_before = jnp.concatenate([jnp.cumsum(counts), jnp.full((1,), n_asg, jnp.int32)])[tile_e]
    spare = n_asg + jnp.arange(R, dtype=jnp.int32) - jnp.repeat(real_before, tmm)
    dst = spare.at[pos].set(jnp.arange(n_asg, dtype=jnp.int32))
    src = jnp.where(dst < n_asg, dst // TOP_K, 0)
    zeros = jnp.zeros((tmm,), jnp.int32)
    before_first = R + jnp.arange(tmm, dtype=jnp.int32)
    src_next = jnp.concatenate([src[tmm:], zeros])
    dst_prev = jnp.concatenate([before_first, dst[:R - tmm]])
    plan = jnp.concatenate([tile_expert, src[:tmm], dst[R - tmm:]]).astype(jnp.int32)
    return plan, src_next, dst_prev, R + tmm


def _rope_tables(T, dk):
    angle = 1.0 / (ROPE_BASE ** jnp.linspace(0.0, 1.0, dk // 2, dtype=F32))
    angle = jnp.repeat(angle, 2)
    ang = jnp.arange(T, dtype=F32)[:, None] * angle[None, :]
    sign = jnp.where(jnp.arange(dk) % 2 == 0, -1.0, 1.0).astype(F32)
    return jnp.cos(ang), jnp.sin(ang) * sign[None, :]


def kernel(x, norm_mix, norm_ffn, norm_final, a_mu, a_wr, a_wk, a_wv, a_wo, a_w0, a_w1, a_w2, a_a0, a_a1, a_a2, a_g1, a_g2, a_kk, a_ka, a_rk, a_gn_w, a_gn_b, b_wq, b_wk, b_wv, b_wg, b_wo, f_w1, f_w3, f_w2, m_router, m_w1, m_w3, m_w2):
    B, T, D = x.shape
    N = B * T
    bf = lambda w: w.astype(BF16)
    vec = lambda w: w.reshape(1, -1).astype(F32)
    h = x.reshape(N, D)

    r, ld, k, v, a, g = _rwkv_proj(
        h, T, vec(norm_mix[0]), a_mu[0], bf(a_wr[0]), bf(a_wk[0]), bf(a_wv[0]), bf(a_w1[0]), bf(a_w2[0]),
        bf(a_a1[0]), bf(a_a2[0]), bf(a_g1[0]), bf(a_g2[0]), vec(a_w0[0]), vec(a_a0[0]))
    y = _rwkv_recur(r, ld, k, v, a, vec(a_kk[0]), vec(a_ka[0]), vec(a_rk[0]), vec(a_gn_w[0]),
                    vec(a_gn_b[0]), B, T)
    h = _mix_out_dense_ffn(h, y, g, bf(a_wo[0]), vec(norm_ffn[0]), bf(f_w1[0]), bf(f_w3[0]), bf(f_w2[0]))

    dk = b_wq.shape[2] // RET_HEADS
    cos, sin = _rope_tables(T, dk)
    q, kr, vr, gr = _ret_proj(h, T, vec(norm_mix[1]), cos, sin, bf(b_wq[0]), bf(b_wk[0]), bf(b_wv[0]),
                              bf(b_wg[0]))
    o = _ret_chunk(q, kr, vr, gr, B, T, RET_HEADS)
    h, u, idx, gate = _ret_out_router(h, o, bf(b_wo[0]), vec(norm_ffn[1]), m_router[0].T.astype(F32))
    plan, src_next, dst_prev, n_rows = _route_plan(idx, N)
    y = _moe_experts(u, plan, src_next, dst_prev, bf(m_w1[0]), bf(m_w3[0]), bf(m_w2[0]), n_rows)
    out = _combine(h, y.reshape(n_rows // TOP_K, TOP_K * D), gate.T, vec(norm_final))
    return out.reshape(B, T, D)
```

```python
import functools
import math

import jax
import jax.numpy as jnp
from jax import lax
from jax.experimental import pallas as pl
from jax.experimental.pallas import tpu as pltpu

F32 = jnp.float32
BF16 = jnp.bfloat16

RMS_EPS = 1e-6
RWKV_HEAD = 64
RWKV_GN_EPS = 64e-5
RET_HEADS = 4
ROPE_BASE = 10000.0
N_EXPERTS = 8
TOP_K = 2

LANES = 128
VMEM_LIMIT = 56 * 1024 * 1024

RWKV_CHUNK = 64
RWKV_SUB = 16
RET_BLOCK = 256
MOE_TILE = 512
MOE_SUB = 256
FFN_SUB = 256
DMA_UNROLL = 8


def _cparams(sem):
    return pltpu.CompilerParams(dimension_semantics=sem, vmem_limit_bytes=VMEM_LIMIT)


def _dot(a, b):
    return jnp.dot(a.astype(BF16), b.astype(BF16), preferred_element_type=F32)


def _dot_nt(a, b):
    return lax.dot_general(a.astype(BF16), b.astype(BF16), (((1,), (1,)), ((), ())),
                           preferred_element_type=F32)


def _dot_tn(a, b):
    return lax.dot_general(a.astype(BF16), b.astype(BF16), (((0,), (0,)), ((), ())),
                           preferred_element_type=F32)


def _rms(x, g):
    return x * lax.rsqrt(jnp.mean(x * x, axis=-1, keepdims=True) + RMS_EPS) * g


def _sigmoid(x):
    return 1.0 / (1.0 + jnp.exp(-x))


def _silu(x):
    return x * _sigmoid(x)


def _full(shape):
    n = len(shape)
    return pl.BlockSpec(shape, lambda *_: (0,) * n, pipeline_mode=pl.Buffered(1))


def _rwkv_proj_kernel(T, tm, h_ref, hp_ref, nw_ref, mu_ref, wr_ref, wk_ref, wv_ref, w1_ref, w2_ref,
                      a1_ref, a2_ref, g1_ref, g2_ref, w0_ref, a0_ref,
                      r_out, ld_out, k_out, v_out, a_out, g_out):
    i = pl.program_id(0)
    nw = nw_ref[...]
    u = _rms(h_ref[...], nw)
    up = _rms(hp_ref[...], nw)
    first = (i * tm) % T == 0
    prev_last = jnp.where(first, 0.0, up[7:8, :])
    row = lax.broadcasted_iota(jnp.int32, u.shape, 0)
    us = jnp.where(row == 0, prev_last, pltpu.roll(u, 1, 0))
    dx = us - u

    def mix(j):
        return (u + dx * mu_ref[j:j + 1, :]).astype(BF16)

    r_out[...] = _dot(mix(0), wr_ref[...]).astype(BF16)
    z =w0_ref[...] + _dot(jnp.tanh(_dot(mix(1), w1_ref[...])), w2_ref[...])
    nz = -z
    softplus = jnp.maximum(nz, 0.0) + jnp.log(1.0 + jnp.exp(-jnp.abs(nz)))
    ld_out[...] = -jnp.exp(-softplus - 0.5)
    k_out[...] = _dot(mix(2), wk_ref[...]).astype(BF16)
    v_out[...] = _dot(mix(3), wv_ref[...]).astype(BF16)
    a_out[...] = _sigmoid(a0_ref[...] + _dot(_dot(mix(4), a1_ref[...]), a2_ref[...])).astype(BF16)
    g_out[...] = _dot(_sigmoid(_dot(mix(5), g1_ref[...])), g2_ref[...]).astype(BF16)


def _rwkv_proj(h, T, nw, mu, wr, wk, wv, w1, w2, a1, a2, g1, g2, w0, a0, tm=512):
    N, D = h.shape
    tm = min(tm, T)
    row = pl.BlockSpec((tm, D), lambda i: (i, 0))
    prev = pl.BlockSpec((8, D), lambda i: (jnp.maximum(i * (tm // 8) - 1, 0), 0))
    ws = [nw, mu, wr, wk, wv, w1, w2, a1, a2, g1, g2, w0, a0]
    out = lambda dt: jax.ShapeDtypeStruct((N, D), dt)
    return pl.pallas_call(
        functools.partial(_rwkv_proj_kernel, T, tm),
        grid=(N // tm,),
        in_specs=[row, prev] + [_full(w.shape) for w in ws],
        out_specs=[row] * 6,
        out_shape=[out(BF16), out(F32), out(BF16), out(BF16), out(BF16), out(BF16)],
        compiler_params=_cparams(("parallel",)),
        name="rwkv_proj",
    )(h, h, *ws)


def _rwkv_pair_chunk(r, ld, cum, k, v, a, kkp, kap, rkp, gnw, gnb, s_ref):
    L = r.shape[0]
    L2 = 2 * L
    lane = lax.broadcasted_iota(jnp.int32, (L, LANES), 1)
    m0 = lane < RWKV_HEAD

    def seg_sum(x):
        s0 = jnp.sum(jnp.where(m0, x, 0.0), axis=-1, keepdims=True)
        s1 = jnp.sum(jnp.where(m0, 0.0, x), axis=-1, keepdims=True)
        return jnp.where(m0, s0, s1)

    def stack(x):
        return jnp.concatenate([jnp.where(m0, x, 0.0), jnp.where(m0, 0.0, x)], axis=0)

    kk = k * kkp
    kk = kk / jnp.maximum(jnp.sqrt(seg_sum(kk * kk)), 1e-12)
    kmod = k * (1.0 + (a - 1.0) * kap)
    aa = -kk
    bb = kk * a

    tot = cum[L - 1:L, :]
    e_cum = jnp.exp(cum)
    e_inv = jnp.exp(-cum)
    e_fin = jnp.exp(tot - cum)
    a_t = stack(aa * jnp.exp(cum - ld))
    r_t = stack(r * e_cum)
    b_t = stack(bb * e_inv)
    k_t = stack(kmod * e_inv)
    b_h = stack(bb * e_fin)
    k_h = stack(kmod * e_fin)
    v_s = stack(v)

    mm = _dot_nt(jnp.concatenate([a_t, r_t], axis=0), jnp.concatenate([b_t, k_t], axis=0))
    yield
    si =lax.broadcasted_iota(jnp.int32, (L2, L2), 0)
    sj = lax.broadcasted_iota(jnp.int32, (L2, L2), 1)
    low = sj < si
    lowi = sj <= si
    m_ab = jnp.where(low, mm[:L2, :L2], 0.0)
    m_ak = jnp.where(low, mm[:L2, L2:], 0.0)
    m_rb = jnp.where(lowi, mm[L2:, :L2], 0.0)
    m_rk = jnp.where(lowi, mm[L2:, L2:], 0.0)

    eye = jnp.where(si == sj, 1.0, 0.0).astype(F32)
    sub_shift = RWKV_SUB.bit_length() - 1
    same = (si >> sub_shift) == (sj >> sub_shift)
    dg = jnp.where(same, m_ab, 0.0)
    og = jnp.where(same, 0.0, m_ab)
    d2 = _dot(dg, dg)
    s0 = s_ref[...]
    ar_s = _dot_nt(jnp.concatenate([a_t, r_t], axis=0), s0)
    c = ar_s[:L2] + _dot(m_ak, v_s)
    yield
    d4 = _dot(d2, d2)
    x1 = eye + dg + d2 + _dot(dg, d2)
    yield
    d8 = _dot(d4, d4)
    yield
    x2 = eye + d4 + d8 + _dot(d4, d8)
    yield
    t_d = _dot(x1, x2)
    yield
    e1 = _dot(t_d, og)
    yield
    e2 = _dot(e1, e1)
    yield
    t_f = eye + e1 + e2 + _dot(e1, e2)
    yield
    t_m = _dot(t_f, t_d)
    yield
    u = _dot(t_m, c)
    yield
    uv = jnp.concatenate([u, v_s], axis=0)
    ys = ar_s[L2:] + _dot(jnp.concatenate([m_rb, m_rk], axis=1), uv)
    s_ref[...] = s0 * jnp.exp(tot) + _dot_tn(uv, jnp.concatenate([b_h, k_h], axis=0))
    yield
    y = ys[:L] + ys[L:]

    inv_n = 1.0 / RWKV_HEAD
    mean = seg_sum(y) * inv_n
    yc = y - mean
    var = seg_sum(yc * yc) * inv_n
    yn = yc * lax.rsqrt(var + RWKV_GN_EPS) * gnw + gnb
    return yn + seg_sum(r * kmod * rkp) * v


def _run_in_lockstep(gens):
    results = [None] * len(gens)
    live = list(range(len(gens)))
    while live:
        still = []
        for i in live:
            try:
                next(gens[i])
                still.append(i)
            except StopIteration as done:
                results[i] = done.value
        live = still
    return results


def _rwkv_recur_kernel(G, L, r_ref, ld_ref, k_ref, v_ref, a_ref, kk_ref, ka_ref, rk_ref, gw_ref, gb_ref,
                       y_ref, s_ref):
    @pl.when(pl.program_id(2) == 0)
    def _():
        s_ref[...] = jnp.zeros_like(s_ref)

    lanes = [slice(g * LANES, (g + 1) * LANES) for g in range(G)]
    ti = lax.broadcasted_iota(jnp.int32, (L, L), 0)
    tj = lax.broadcasted_iota(jnp.int32, (L, L), 1)
    tri = jnp.where(tj <= ti, 1.0, 0.0).astype(BF16)

    def chunk(c, carry):
        rows = pl.ds(pl.multiple_of(c * L, L), L)
        ld = ld_ref[rows, :]
        cum = jnp.zeros_like(ld)
        rest = ld
        for _ in range(3):
            part = rest.astype(BF16)
            cum = cum + jnp.dot(tri, part, preferred_element_type=F32)
            rest = rest - part.astype(F32)
        f32 = lambda ref, sl: ref[rows, sl].astype(F32)
        ys = _run_in_lockstep([
            _rwkv_pair_chunk(f32(r_ref, sl), ld[:, sl], cum[:, sl], f32(k_ref, sl), f32(v_ref, sl),
                             f32(a_ref, sl), kk_ref[:, sl], ka_ref[:, sl], rk_ref[:, sl], gw_ref[:, sl],
                             gb_ref[:, sl], s_ref.at[g])
            for g, sl in enumerate(lanes)])
        for sl, y in zip(lanes, ys):
            y_ref[rows, sl] = y.astype(BF16)
        return carry

    lax.fori_loop(0, r_ref.shape[0] // L, chunk, 0)


def _rwkv_recur(r, ld, k, v, a, kkp, kap, rkp, gnw, gnb, B, T, G=8, chunks_per_step=4):
    N, D = r.shape
    L = RWKV_CHUNK
    W = G * LANES
    rows = L * chunks_per_step
    nc = T // rows
    seq = pl.BlockSpec((rows, W), lambda b, p, c: (b * nc + c, p))
    par = pl.BlockSpec((1, W), lambda b, p, c: (0, p))
    return pl.pallas_call(
        functools.partial(_rwkv_recur_kernel, G, L),
        grid=(B, D // W, nc),
        in_specs=[seq] * 5 + [par] * 5,
        out_specs=seq,
        out_shape=jax.ShapeDtypeStruct((N, D), BF16),
        scratch_shapes=[pltpu.VMEM((G, LANES, LANES), F32)],
        compiler_params=_cparams(("parallel", "parallel", "arbitrary")),
        name="rwkv_recur",
    )(r, ld, k, v, a, kkp, kap, rkp, gnw, gnb)


def _swiglu_chunk(u, w1, w3, w2):
    return _dot(_silu(_dot(u, w1)) * _dot(u, w3), w2)


def _dense_ffn_kernel(h_ref, y_ref, g_ref, wo_ref, nw_ref, w1_ref, w3_ref, w2_ref, o_ref, u_ref):
    h = h_ref[...] + _dot(y_ref[...].astype(F32) * g_ref[...].astype(F32), wo_ref[...])
    u_ref[...] = _rms(h, nw_ref[...]).astype(BF16)
    o_ref[...] = h
    for c in range(w1_ref.shape[1] // FFN_SUB):
        cs = slice(c * FFN_SUB, (c + 1) * FFN_SUB)
        o_ref[...] += _swiglu_chunk(u_ref[...], w1_ref[:, cs], w3_ref[:, cs], w2_ref[cs, :])


def _mix_out_dense_ffn(h, y, g, wo, nw, w1, w3, w2, tm=512):
    N, D = h.shape
    tm = min(tm, N)
    row = pl.BlockSpec((tm, D), lambda i: (i, 0))
    return pl.pallas_call(
        _dense_ffn_kernel,
        grid=(N // tm,),
        in_specs=[row, row, row] + [_full(w.shape) for w in (wo, nw, w1, w3, w2)],
        out_specs=row,
        out_shape=jax.ShapeDtypeStruct((N, D), F32),
        scratch_shapes=[pltpu.VMEM((tm, D), BF16)],
        compiler_params=_cparams(("parallel",)),
        name="dense_ffn",
    )(h, y, g, wo, nw, w1, w3, w2)


def _ret_proj_kernel(dk, h_ref, nw_ref, cos_ref, sin_ref, wq_ref, wk_ref, wv_ref, wg_ref,
                     q_out, k_out, v_out, g_out):
    u = _rms(h_ref[...], nw_ref[...]).astype(BF16)
    nh = wq_ref.shape[1] // dk
    cos = jnp.concatenate([cos_ref[...]] * nh, axis=1)
    sin = jnp.concatenate([sin_ref[...]] * nh, axis=1)
    lane = lax.broadcasted_iota(jnp.int32, cos.shape, 1)
    even = (lane & 1) == 0
    width = cos.shape[1]

    def rope(t):
        nxt = pltpu.roll(t, width - 1, 1)
        prv = pltpu.roll(t, 1, 1)
        return t * cos + jnp.where(even, nxt, prv) * sin

    q_out[...] = rope(_dot(u, wq_ref[...])).astype(BF16)
    k_out[...] = rope(_dot(u, wk_ref[...]) * (dk ** -0.5)).astype(BF16)
    v_out[...] = _dot(u, wv_ref[...]).astype(BF16)
    g_out[...] = _dot(u, wg_ref[...]).astype(BF16)


def _ret_proj(h, T, nw, cos, sin, wq, wk, wv, wg, tm=512):
    N, D = h.shape
    dk = cos.shape[1]
    tm = min(tm, T)
    tpb = T // tm
    row = lambda w: pl.BlockSpec((tm, w), lambda i: (i, 0))
    tab = pl.BlockSpec((tm, dk), lambda i: (i % tpb, 0))
    return pl.pallas_call(
        functools.partial(_ret_proj_kernel, dk),
        grid=(N // tm,),
        in_specs=[row(D), _full(nw.shape), tab, tab] + [_full(w.shape) for w in (wq, wk, wv, wg)],
        out_specs=[row(wq.shape[1]), row(wk.shape[1]), row(wv.shape[1]), row(wg.shape[1])],
        out_shape=[jax.ShapeDtypeStruct((N, w.shape[1]), BF16) for w in (wq, wk, wv, wg)],
        compiler_params=_cparams(("parallel",)),
        name="ret_proj",
    )(h, nw, cos, sin, wq, wk, wv, wg)


def _ret_head_chunk(q, k, v, g, intra, qd, kd, cd, r_ref):
    r0 = r_ref[...]
    s = _dot_nt(q, k) * intra
    inter = _dot(q, r0) * qd
    r_ref[...] = r0 * cd + _dot_tn(k.astype(F32) * kd, v)
    yield
    o = _dot(s, v) + inter
    yield
    o = o * lax.rsqrt(jnp.mean(o * o, axis=-1, keepdims=True) + RMS_EPS)
    return (_silu(g.astype(F32)) * o).astype(BF16)


def _ret_chunk_kernel(H, q_ref, k_ref, v_ref, g_ref, intra_ref, qd_ref, kd_ref, cd_ref, o_ref, r_ref):
    @pl.when(pl.program_id(1) == 0)
    def _():
        r_ref[...] = jnp.zeros_like(r_ref)

    dk = q_ref.shape[1] // H
    dv = v_ref.shape[1] // H
    ks = [slice(h * dk, (h + 1) * dk) for h in range(H)]
    vs = [slice(h * dv, (h + 1) * dv) for h in range(H)]
    outs = _run_in_lockstep([
        _ret_head_chunk(q_ref[:, ks[h]], k_ref[:, ks[h]], v_ref[:, vs[h]], g_ref[:, vs[h]],
                        intra_ref[h], qd_ref[h], kd_ref[h], cd_ref[h], r_ref.at[h])
        for h in range(H)])
    for h in range(H):
        o_ref[:, vs[h]] = outs[h]


def _ret_chunk(q, k, v, g, B, T, H):
    N = q.shape[0]
    dk = q.shape[1] // H
    dv = v.shape[1] // H
    C = min(RET_BLOCK, T)
    nc = T // C
    lg = jnp.log(1.0 - 2.0 ** (-5.0 - jnp.arange(H, dtype=F32)))
    n = jnp.arange(C, dtype=F32)
    diff = n[:, None] - n[None, :]
    intra = jnp.where(diff >= 0, jnp.exp(lg[:, None, None] * jnp.maximum(diff, 0.0)), 0.0)
    qd = jnp.broadcast_to(jnp.exp(lg[:, None] * (n[None, :] + 1.0))[:, :, None], (H, C, LANES))
    kd = jnp.broadcast_to(jnp.exp(lg[:, None] * (C - 1.0 - n[None, :]))[:, :, None], (H, C, LANES))
    cd = jnp.broadcast_to(jnp.exp(lg * C)[:, None, None], (H, 8, LANES))
    qd = jnp.concatenate([qd] * (dv // LANES), axis=2)
    kd = jnp.concatenate([kd] * (dk // LANES), axis=2)
    cd = jnp.concatenate([cd] * (dv // LANES), axis=2)[:, :1]
    seq = lambda w: pl.BlockSpec((C, w), lambda b, c: (b * nc + c, 0))
    return pl.pallas_call(
        functools.partial(_ret_chunk_kernel, H),
        grid=(B, nc),
        in_specs=[seq(H * dk), seq(H * dk), seq(H * dv), seq(H * dv)]
        + [_full(t.shape) for t in (intra, qd, kd, cd)],
        out_specs=seq(H * dv),
        out_shape=jax.ShapeDtypeStruct((N, H * dv), BF16),
        scratch_shapes=[pltpu.VMEM((H, dk, dv), F32)],
        compiler_params=_cparams(("parallel", "arbitrary")),
        name="ret_chunk",
    )(q, k, v, g, intra, qd, kd, cd)


def _ret_out_router_kernel(h_ref, o_ref, wo_ref, nw_ref, rt_ref, h_out, u_out, idx_out, gate_out):
    h = h_ref[...] + _dot(o_ref[...], wo_ref[...])
    h_out[...] = h
    u = _rms(h, nw_ref[...])
    u_out[...] = u
    logits = lax.dot_general(rt_ref[...], u, (((1,), (1,)), ((), ())),
                             precision=lax.Precision.HIGHEST, preferred_element_type=F32)
    e_iota = lax.broadcasted_iota(jnp.int32, logits.shape, 0)
    big = jnp.int32(N_EXPERTS)
    m1 = jnp.max(logits, axis=0, keepdims=True)
    i1 = jnp.min(jnp.where(logits == m1, e_iota, big), axis=0, keepdims=True)
    rest = jnp.where(e_iota == i1, -jnp.inf, logits)
    m2 = jnp.max(rest, axis=0, keepdims=True)
    i2 = jnp.min(jnp.where(rest == m2, e_iota, big), axis=0, keepdims=True)
    p2 = jnp.exp(m2 - m1)
    g1 = 1.0 / (1.0 + p2)
    idx_out[...] = jnp.concatenate([i1, i2], axis=0)
    gate_out[...] = jnp.concatenate([g1, p2 * g1], axis=0)


def _ret_out_router(h, o, wo, nw, router_t, tm=512):
    N, D = h.shape
    K = o.shape[1]
    tm = min(tm, N)
    row = lambda w: pl.BlockSpec((tm, w), lambda i: (i, 0))
    col = pl.BlockSpec((TOP_K, tm), lambda i: (0, i))
    return pl.pallas_call(
        _ret_out_router_kernel,
        grid=(N // tm,),
        in_specs=[row(D), row(K), _full(wo.shape), _full(nw.shape), _full(router_t.shape)],
        out_specs=[row(D), row(D), col, col],
        out_shape=[jax.ShapeDtypeStruct((N, D), F32), jax.ShapeDtypeStruct((N, D), F32),
                   jax.ShapeDtypeStruct((TOP_K, N), jnp.int32), jax.ShapeDtypeStruct((TOP_K, N), F32)],
        compiler_params=_cparams(("parallel",)),
        name="ret_out_router",
    )(h, o, wo, nw, router_t)


def _row_copy(src_hbm, dst_ref, sem, s, j):
    return pltpu.make_async_copy(src_hbm.at[pl.ds(s, 1), :], dst_ref.at[pl.ds(j, 1), :], sem)


def _moe_kernel(tmm, q, te_ref, src_ref, dst_ref, u_hbm, w1_ref, w3_ref, w2_ref, y_hbm,
                xbuf, xb_ref, acc_ref, obuf, gsem, ssem):
    i = pl.program_id(0)
    f = pl.program_id(1)
    n_i = pl.num_programs(0)
    n_f = pl.num_programs(1)
    cur = i % 2
    oth = 1 - cur

    def gather(slot, j, tok):
        return _row_copy(u_hbm, xbuf.at[slot], gsem.at[slot], tok, j)

    def scatter(slot, j, row):
        return pltpu.make_async_copy(obuf.at[slot].at[pl.ds(j, 1), :], y_hbm.at[pl.ds(row, 1), :],
                                     ssem.at[slot])

    def wait_rows(make):
        def body(j, c):
            make(j).wait()
            return c
        lax.fori_loop(0, tmm, body, 0, unroll=DMA_UNROLL)

    @pl.when((i == 0) & (f == 0))
    def _():
        obuf[1] = jnp.zeros((tmm, obuf.shape[2]), F32)

        def first(j, c):
            gather(0, j, te_ref[n_i + j]).start()
            return c
        lax.fori_loop(0, tmm, first, 0, unroll=DMA_UNROLL)

    @pl.when(f == 0)
    def _():
        wait_rows(lambda j: gather(cur, j, 0))
        xb_ref[...] = xbuf[cur].astype(BF16)
        acc_ref[...] = jnp.zeros_like(acc_ref)

    def issue_rows(slot, first):
        for j in range(first, first + q):
            gather(slot, j, src_ref[j]).start()
            scatter(slot, j, dst_ref[j]).start()

    for slot in range(2):
        for ff in range(tmm // q):
            pl.when((oth == slot) & (f == ff))(functools.partial(issue_rows, slot, ff * q))

    for c in range(w1_ref.shape[2] // MOE_SUB):
        cs = slice(c * MOE_SUB, (c + 1) * MOE_SUB)
        acc_ref[...] += _swiglu_chunk(xb_ref[...], w1_ref[0, :, cs], w3_ref[0, :, cs], w2_ref[0, cs, :])

    @pl.when(f == n_f - 1)
    def _():
        @pl.when(i > 0)
        def _():
            wait_rows(lambda j: scatter(cur, j, 0))
        obuf[cur] = acc_ref[...]

        @pl.when(i == n_i - 1)
        def _():
            def last(j, c):
                scatter(cur, j, te_ref[n_i + tmm + j]).start()
                return c
            lax.fori_loop(0, tmm, last, 0, unroll=DMA_UNROLL)
            wait_rows(lambda j: scatter(oth, j, 0))
            wait_rows(lambda j: scatter(cur, j, 0))
            wait_rows(lambda j: gather(oth, j, 0))


def _moe_experts(u, plan, src_next, dst_prev, w1, w3, w2, n_rows_out, tf=1792):
    N, D = u.shape
    Fd = w1.shape[2]
    tmm = MOE_TILE
    n_tiles = src_next.shape[0] // tmm
    nf = Fd // tf
    q = tmm // nf
    smem = lambda: pl.BlockSpec((tmm,), lambda i, f, te: (i,), memory_space=pltpu.SMEM)
    gs = pltpu.PrefetchScalarGridSpec(
        num_scalar_prefetch=1,
        grid=(n_tiles, nf),
        in_specs=[smem(), smem(), pl.BlockSpec(memory_space=pl.ANY),
                  pl.BlockSpec((1, D, tf), lambda i, f, te: (te[i], 0, f)),
                  pl.BlockSpec((1, D, tf), lambda i, f, te: (te[i], 0, f)),
                  pl.BlockSpec((1, tf, D), lambda i, f, te: (te[i], f, 0))],
        out_specs=pl.BlockSpec(memory_space=pl.ANY),
        scratch_shapes=[pltpu.VMEM((2, tmm, D), F32), pltpu.VMEM((tmm, D), BF16), pltpu.VMEM((tmm, D), F32),
                        pltpu.VMEM((2, tmm, D), F32), pltpu.SemaphoreType.DMA((2,)),
                        pltpu.SemaphoreType.DMA((2,))],
    )
    return pl.pallas_call(
        functools.partial(_moe_kernel, tmm, q),
        grid_spec=gs,
        out_shape=jax.ShapeDtypeStruct((n_rows_out, D), F32),
        compiler_params=_cparams(("arbitrary", "arbitrary")),
        name="moe_experts",
    )(plan, src_next, dst_prev, u, w1, w3, w2)


def _combine_kernel(h_ref, y0_ref, y1_ref, gate_ref, nw_ref, o_ref):
    gate = gate_ref[...]
    moe = gate[:, 0:1] * y0_ref[...] + gate[:, 1:2] * y1_ref[...]
    o_ref[...] = _rms(h_ref[...] + moe, nw_ref[...])


def _combine(h, y, gate, nw, tc=512):
    N, D = h.shape
    tc = min(tc, N)
    nb = N // tc
    return pl.pallas_call(
        _combine_kernel,
        grid=(nb,),
        in_specs=[pl.BlockSpec((tc, D), lambda i: (i, 0)),
                  pl.BlockSpec((tc, D), lambda i: (i, 0)),
                  pl.BlockSpec((tc, D), lambda i: (nb + i, 0)),
                  pl.BlockSpec((tc, TOP_K), lambda i: (i, 0)),
                  _full(nw.shape)],
        out_specs=pl.BlockSpec((tc, D), lambda i: (i, 0)),
        out_shape=jax.ShapeDtypeStruct((N, D), F32),
        compiler_params=_cparams(("parallel",)),
        name="moe_combine",
    )(h, y, y, gate, nw)


def _route_plan(idx, N):
    tmm = MOE_TILE
    n_asg = TOP_K * N
    n_tiles = n_asg // tmm + N_EXPERTS
    R = n_tiles * tmm
    flat_e = idx.reshape(-1)
    onehot = (flat_e[:, None] == jnp.arange(N_EXPERTS, dtype=jnp.int32)[None, :]).astype(jnp.int32)
    csum = jnp.cumsum(onehot, axis=0)
    rank = jnp.sum(csum * onehot, axis=1) - 1
    counts = csum[-1]
    ptiles = (counts + tmm - 1) // tmm
    tile_end = jnp.cumsum(ptiles)
    tile_start = tile_end - ptiles
    pos = tile_start[flat_e] * tmm + rank
    t = jnp.arange(n_tiles, dtype=jnp.int32)
    tile_e = jnp.sum((t[:, None] >= tile_end[None, :]).astype(jnp.int32), axis=1)
    tile_expert = jnp.minimum(tile_e, N_EXPERTS - 1)
    real_before = jnp.concatenate([jnp.cumsum(counts), jnp.full((1,), n_asg, jnp.int32)])[tile_e]
    spare = n_asg + jnp.arange(R, dtype=jnp.int32) - jnp.repeat(real_before, tmm)
    dst = spare.at[pos].set(jnp.arange(n_asg, dtype=jnp.int32))
    src = jnp.where(dst < n_asg, dst % N, 0)
    zeros = jnp.zeros((tmm,), jnp.int32)
    before_first = R + jnp.arange(tmm, dtype=jnp.int32)
    src_next = jnp.concatenate([src[tmm:], zeros])
    dst_prev = jnp.concatenate([before_first, dst[:R - tmm]])
    plan = jnp.concatenate([tile_expert, src[:tmm], dst[R - tmm:]]).astype(jnp.int32)
    return plan, src_next, dst_prev, R + tmm


def _rope_tables(T, dk):
    angle = 1.0 / (ROPE_BASE ** jnp.linspace(0.0, 1.0, dk // 2, dtype=F32))
    angle = jnp.repeat(angle, 2)
    ang = jnp.arange(T, dtype=F32)[:, None] * angle[None, :]
    sign = jnp.where(jnp.arange(dk) % 2 == 0, -1.0, 1.0).astype(F32)
    return jnp.cos(ang), jnp.sin(ang) * sign[None, :]


def kernel(x, norm_mix, norm_ffn, norm_final, a_mu, a_wr, a_wk, a_wv, a_wo, a_w0, a_w1, a_w2, a_a0, a_a1, a_a2, a_g1, a_g2, a_kk, a_ka, a_rk, a_gn_w, a_gn_b, b_wq, b_wk, b_wv, b_wg, b_wo, f_w1, f_w3, f_w2, m_router, m_w1, m_w3, m_w2):
    B, T, D = x.shape
    N = B * T
    bf = lambda w: w.astype(BF16)
    vec = lambda w: w.reshape(1, -1).astype(F32)
    h = x.reshape(N, D)

    r, ld, k, v, a, g = _rwkv_proj(
        h, T, vec(norm_mix[0]), a_mu[0], bf(a_wr[0]), bf(a_wk[0]), bf(a_wv[0]), bf(a_w1[0]), bf(a_w2[0]),
        bf(a_a1[0]), bf(a_a2[0]), bf(a_g1[0]), bf(a_g2[0]), vec(a_w0[0]), vec(a_a0[0]))
    y = _rwkv_recur(r, ld, k, v, a, vec(a_kk[0]), vec(a_ka[0]), vec(a_rk[0]), vec(a_gn_w[0]),
                    vec(a_gn_b[0]), B, T)
    h = _mix_out_dense_ffn(h, y, g, bf(a_wo[0]), vec(norm_ffn[0]), bf(f_w1[0]), bf(f_w3[0]), bf(f_w2[0]))

    dk = b_wq.shape[2] // RET_HEADS
    cos, sin = _rope_tables(T, dk)
    q, kr, vr, gr = _ret_proj(h, T, vec(norm_mix[1]), cos, sin, bf(b_wq[0]), bf(b_wk[0]), bf(b_wv[0]),
                              bf(b_wg[0]))
    o = _ret_chunk(q, kr, vr, gr, B, T, RET_HEADS)
    h, u, idx, gate = _ret_out_router(h, o, bf(b_wo[0]), vec(norm_ffn[1]), m_router[0].T.astype(F32))
    plan, src_next, dst_prev, n_rows = _route_plan(idx, N)
    y = _moe_experts(u, plan, src_next, dst_prev, bf(m_w1[0]), bf(m_w3[0]), bf(m_w2[0]), n_rows)
    out = _combine(h, y, gate.T, vec(norm_final))
    return out.reshape(B, T, D)
```

```python
import functools
import math

import jax
import jax.numpy as jnp
from jax import lax
from jax.experimental import pallas as pl
from jax.experimental.pallas import tpu as pltpu

F32 = jnp.float32
BF16 = jnp.bfloat16

RMS_EPS = 1e-6
RWKV_HEAD = 64
RWKV_GN_EPS = 64e-5
RET_HEADS = 4
ROPE_BASE = 10000.0
N_EXPERTS = 8
TOP_K = 2

LANES = 128
VMEM_LIMIT = 56 * 1024 * 1024

RWKV_CHUNK = 64
RWKV_SUB = 16
RET_BLOCK = 256
MOE_TILE = 512
MOE_SUB = 256
FFN_SUB = 256
DMA_UNROLL = 8


def _cparams(sem):
    return pltpu.CompilerParams(dimension_semantics=sem, vmem_limit_bytes=VMEM_LIMIT)


def _dot(a, b):
    return jnp.dot(a.astype(BF16), b.astype(BF16), preferred_element_type=F32)


def _dot_nt(a, b):
    return lax.dot_general(a.astype(BF16), b.astype(BF16), (((1,), (1,)), ((), ())),
                           preferred_element_type=F32)


def _dot_tn(a, b):
    return lax.dot_general(a.astype(BF16), b.astype(BF16), (((0,), (0,)), ((), ())),
                           preferred_element_type=F32)


def _rms(x, g):
    return x * lax.rsqrt(jnp.mean(x * x, axis=-1, keepdims=True) + RMS_EPS) * g


def _sigmoid(x):
    return 1.0 / (1.0 + jnp.exp(-x))


def _silu(x):
    return x * _sigmoid(x)


def _full(shape):
    n = len(shape)
    return pl.BlockSpec(shape, lambda *_: (0,) * n, pipeline_mode=pl.Buffered(1))


def _rwkv_proj_kernel(T, tm, h_ref, hp_ref, nw_ref, mu_ref, wr_ref, wk_ref, wv_ref, w1_ref, w2_ref,
                      a1_ref, a2_ref, g1_ref, g2_ref, w0_ref, a0_ref,
                      r_out, ld_out, k_out, v_out, a_out, g_out):
    i = pl.program_id(0)
    nw = nw_ref[...]
    u = _rms(h_ref[...], nw)
    up = _rms(hp_ref[...], nw)
    first = (i * tm) % T == 0
    prev_last = jnp.where(first, 0.0, up[7:8, :])
    row = lax.broadcasted_iota(jnp.int32, u.shape, 0)
    us = jnp.where(row == 0, prev_last, pltpu.roll(u, 1, 0))
    dx = us - u

    def mix(j):
        return (u + dx * mu_ref[j:j + 1, :]).astype(BF16)

    r_out[...] = _dot(mix(0), wr_ref[...]).astype(BF16)
    z =w0_ref[...] + _dot(jnp.tanh(_dot(mix(1), w1_ref[...])), w2_ref[...])
    nz = -z
    softplus = jnp.maximum(nz, 0.0) + jnp.log(1.0 + jnp.exp(-jnp.abs(nz)))
    ld_out[...] = -jnp.exp(-softplus - 0.5)
    k_out[...] = _dot(mix(2), wk_ref[...]).astype(BF16)
    v_out[...] = _dot(mix(3), wv_ref[...]).astype(BF16)
    a_out[...] = _sigmoid(a0_ref[...] + _dot(_dot(mix(4), a1_ref[...]), a2_ref[...])).astype(BF16)
    g_out[...] = _dot(_sigmoid(_dot(mix(5), g1_ref[...])), g2_ref[...]).astype(BF16)


def _rwkv_proj(h, T, nw, mu, wr, wk, wv, w1, w2, a1, a2, g1, g2, w0, a0, tm=512):
    N, D = h.shape
    tm = min(tm, T)
    row = pl.BlockSpec((tm, D), lambda i: (i, 0))
    prev = pl.BlockSpec((8, D), lambda i: (jnp.maximum(i * (tm // 8) - 1, 0), 0))
    ws = [nw, mu, wr, wk, wv, w1, w2, a1, a2, g1, g2, w0, a0]
    out = lambda dt: jax.ShapeDtypeStruct((N, D), dt)
    return pl.pallas_call(
        functools.partial(_rwkv_proj_kernel, T, tm),
        grid=(N // tm,),
        in_specs=[row, prev] + [_full(w.shape) for w in ws],
        out_specs=[row] * 6,
        out_shape=[out(BF16), out(F32), out(BF16), out(BF16), out(BF16), out(BF16)],
        compiler_params=_cparams(("parallel",)),
        name="rwkv_proj",
    )(h, h, *ws)


def _rwkv_pair_chunk(r, ld, cum, k, v, a, kkp, kap, rkp, gnw, gnb, s_ref):
    L = r.shape[0]
    L2 = 2 * L
    lane = lax.broadcasted_iota(jnp.int32, (L, LANES), 1)
    m0 = lane < RWKV_HEAD

    def seg_sum(x):
        s0 = jnp.sum(jnp.where(m0, x, 0.0), axis=-1, keepdims=True)
        s1 = jnp.sum(jnp.where(m0, 0.0, x), axis=-1, keepdims=True)
        return jnp.where(m0, s0, s1)

    def stack(x):
        return jnp.concatenate([jnp.where(m0, x, 0.0), jnp.where(m0, 0.0, x)], axis=0)

    kk = k * kkp
    kk = kk / jnp.maximum(jnp.sqrt(seg_sum(kk * kk)), 1e-12)
    kmod = k * (1.0 + (a - 1.0) * kap)
    aa = -kk
    bb = kk * a

    tot = cum[L - 1:L, :]
    e_cum = jnp.exp(cum)
    e_inv = jnp.exp(-cum)
    e_fin = jnp.exp(tot - cum)
    a_t = stack(aa * jnp.exp(cum - ld))
    r_t = stack(r * e_cum)
    b_t = stack(bb * e_inv)
    k_t = stack(kmod * e_inv)
    b_h = stack(bb * e_fin)
    k_h = stack(kmod * e_fin)
    v_s = stack(v)

    mm = _dot_nt(jnp.concatenate([a_t, r_t], axis=0), jnp.concatenate([b_t, k_t], axis=0))
    yield
    si =lax.broadcasted_iota(jnp.int32, (L2, L2), 0)
    sj = lax.broadcasted_iota(jnp.int32, (L2, L2), 1)
    low = sj < si
    lowi = sj <= si
    m_ab = jnp.where(low, mm[:L2, :L2], 0.0)
    m_ak = jnp.where(low, mm[:L2, L2:], 0.0)
    m_rb = jnp.where(lowi, mm[L2:, :L2], 0.0)
    m_rk = jnp.where(lowi, mm[L2:, L2:], 0.0)

    eye = jnp.where(si == sj, 1.0, 0.0).astype(F32)
    sub_shift = RWKV_SUB.bit_length() - 1
    same = (si >> sub_shift) == (sj >> sub_shift)
    dg = jnp.where(same, m_ab, 0.0)
    og = jnp.where(same, 0.0, m_ab)
    d2 = _dot(dg, dg)
    s0 = s_ref[...]
    ar_s = _dot_nt(jnp.concatenate([a_t, r_t], axis=0), s0)
    c = ar_s[:L2] + _dot(m_ak, v_s)
    yield
    d4 = _dot(d2, d2)
    x1 = eye + dg + d2 + _dot(dg, d2)
    yield
    d8 = _dot(d4, d4)
    yield
    x2 = eye + d4 + d8 + _dot(d4, d8)
    yield
    t_d = _dot(x1, x2)
    yield
    e1 = _dot(t_d, og)
    yield
    e2 = _dot(e1, e1)
    yield
    t_f = eye + e1 + e2 + _dot(e1, e2)
    yield
    t_m = _dot(t_f, t_d)
    yield
    u = _dot(t_m, c)
    yield
    uv = jnp.concatenate([u, v_s], axis=0)
    ys = ar_s[L2:] + _dot(jnp.concatenate([m_rb, m_rk], axis=1), uv)
    s_ref[...] = s0 * jnp.exp(tot) + _dot_tn(uv, jnp.concatenate([b_h, k_h], axis=0))
    yield
    y = ys[:L] + ys[L:]

    inv_n = 1.0 / RWKV_HEAD
    mean = seg_sum(y) * inv_n
    yc = y - mean
    var = seg_sum(yc * yc) * inv_n
    yn = yc * lax.rsqrt(var + RWKV_GN_EPS) * gnw + gnb
    return yn + seg_sum(r * kmod * rkp) * v


def _run_in_lockstep(gens):
    results = [None] * len(gens)
    live = list(range(len(gens)))
    while live:
        still = []
        for i in live:
            try:
                next(gens[i])
                still.append(i)
            except StopIteration as done:
                results[i] = done.value
        live = still
    return results


def _rwkv_recur_kernel(G, L, r_ref, ld_ref, k_ref, v_ref, a_ref, kk_ref, ka_ref, rk_ref, gw_ref, gb_ref,
                       y_ref, s_ref):
    @pl.when(pl.program_id(0) == 0)
    def _():
        s_ref[...] = jnp.zeros_like(s_ref)

    B = r_ref.shape[0]
    blocks = [(b, slice(g * LANES, (g + 1) * LANES)) for b in range(B) for g in range(G)]
    ti = lax.broadcasted_iota(jnp.int32, (L, L), 0)
    tj = lax.broadcasted_iota(jnp.int32, (L, L), 1)
    tri = jnp.where(tj <= ti, 1.0, 0.0).astype(BF16)

    def chunk(c, carry):
        rows = pl.ds(pl.multiple_of(c * L, L), L)
        lds, cums = [], []
        for b in range(B):
            ld = ld_ref[b, rows, :]
            cum = jnp.zeros_like(ld)
            rest = ld
            for _ in range(3):
                part = rest.astype(BF16)
                cum = cum + jnp.dot(tri, part, preferred_element_type=F32)
                rest = rest - part.astype(F32)
            lds.append(ld)
            cums.append(cum)
        f32 = lambda ref, b, sl: ref[b, rows, sl].astype(F32)
        ys = _run_in_lockstep([
            _rwkv_pair_chunk(f32(r_ref, b, sl), lds[b][:, sl], cums[b][:, sl], f32(k_ref, b, sl),
                             f32(v_ref, b, sl), f32(a_ref, b, sl), kk_ref[:, sl], ka_ref[:, sl],
                             rk_ref[:, sl], gw_ref[:, sl], gb_ref[:, sl], s_ref.at[n])
            for n, (b, sl) in enumerate(blocks)])
        for (b, sl), y in zip(blocks, ys):
            y_ref[b, rows, sl] = y.astype(BF16)
        return carry

    lax.fori_loop(0, r_ref.shape[1] // L, chunk, 0)


def _rwkv_recur(r, ld, k, v, a, kkp, kap, rkp, gnw, gnb, B, T, chunks_per_step=4):
    N, D = r.shape
    L = RWKV_CHUNK
    G = D // LANES
    rows = L * chunks_per_step
    seq = pl.BlockSpec((B, rows, D), lambda c: (0, c, 0))
    par = pl.BlockSpec((1, D), lambda c: (0, 0))
    b3 = lambda t: t.reshape(B, T, D)
    y = pl.pallas_call(
        functools.partial(_rwkv_recur_kernel, G, L),
        grid=(T // rows,),
        in_specs=[seq] * 5 + [par] * 5,
        out_specs=seq,
        out_shape=jax.ShapeDtypeStruct((B, T, D), BF16),
        scratch_shapes=[pltpu.VMEM((B * G, LANES, LANES), F32)],
        compiler_params=_cparams(("arbitrary",)),
        name="rwkv_recur",
    )(b3(r), b3(ld), b3(k), b3(v), b3(a), kkp, kap, rkp, gnw, gnb)
    return y.reshape(N, D)


def _swiglu_chunk(u, w1, w3, w2):
    return _dot(_silu(_dot(u, w1)) * _dot(u, w3), w2)


def _dense_ffn_kernel(h_ref, y_ref, g_ref, wo_ref, nw_ref, w1_ref, w3_ref, w2_ref, o_ref, u_ref):
    h = h_ref[...] + _dot(y_ref[...].astype(F32) * g_ref[...].astype(F32), wo_ref[...])
    u_ref[...] = _rms(h, nw_ref[...]).astype(BF16)
    o_ref[...] = h
    for c in range(w1_ref.shape[1] // FFN_SUB):
        cs = slice(c * FFN_SUB, (c + 1) * FFN_SUB)
        o_ref[...] += _swiglu_chunk(u_ref[...], w1_ref[:, cs], w3_ref[:, cs], w2_ref[cs, :])


def _mix_out_dense_ffn(h, y, g, wo, nw, w1, w3, w2, tm=512):
    N, D = h.shape
    tm = min(tm, N)
    row = pl.BlockSpec((tm, D), lambda i: (i, 0))
    return pl.pallas_call(
        _dense_ffn_kernel,
        grid=(N // tm,),
        in_specs=[row, row, row] + [_full(w.shape) for w in (wo, nw, w1, w3, w2)],
        out_specs=row,
        out_shape=jax.ShapeDtypeStruct((N, D), F32),
        scratch_shapes=[pltpu.VMEM((tm, D), BF16)],
        compiler_params=_cparams(("parallel",)),
        name="dense_ffn",
    )(h, y, g, wo, nw, w1, w3, w2)


def _ret_proj_kernel(dk, h_ref, nw_ref, cos_ref, sin_ref, wq_ref, wk_ref, wv_ref, wg_ref,
                     q_out, k_out, v_out, g_out):
    u = _rms(h_ref[...], nw_ref[...]).astype(BF16)
    nh = wq_ref.shape[1] // dk
    cos = jnp.concatenate([cos_ref[...]] * nh, axis=1)
    sin = jnp.concatenate([sin_ref[...]] * nh, axis=1)
    lane = lax.broadcasted_iota(jnp.int32, cos.shape, 1)
    even = (lane & 1) == 0
    width = cos.shape[1]

    def rope(t):
        nxt = pltpu.roll(t, width - 1, 1)
        prv = pltpu.roll(t, 1, 1)
        return t * cos + jnp.where(even, nxt, prv) * sin

    q_out[...] = rope(_dot(u, wq_ref[...])).astype(BF16)
    k_out[...] = rope(_dot(u, wk_ref[...]) * (dk ** -0.5)).astype(BF16)
    v_out[...] = _dot(u, wv_ref[...]).astype(BF16)
    g_out[...] = _dot(u, wg_ref[...]).astype(BF16)


def _ret_proj(h, T, nw, cos, sin, wq, wk, wv, wg, tm=512):
    N, D = h.shape
    dk = cos.shape[1]
    tm = min(tm, T)
    tpb = T // tm
    row = lambda w: pl.BlockSpec((tm, w), lambda i: (i, 0))
    tab = pl.BlockSpec((tm, dk), lambda i: (i % tpb, 0))
    return pl.pallas_call(
        functools.partial(_ret_proj_kernel, dk),
        grid=(N // tm,),
        in_specs=[row(D), _full(nw.shape), tab, tab] + [_full(w.shape) for w in (wq, wk, wv, wg)],
        out_specs=[row(wq.shape[1]), row(wk.shape[1]), row(wv.shape[1]), row(wg.shape[1])],
        out_shape=[jax.ShapeDtypeStruct((N, w.shape[1]), BF16) for w in (wq, wk, wv, wg)],
        compiler_params=_cparams(("parallel",)),
        name="ret_proj",
    )(h, nw, cos, sin, wq, wk, wv, wg)


def _ret_head_chunk(q, k, v, g, intra, qd, kd, cd, r_ref):
    r0 = r_ref[...]
    s = _dot_nt(q, k) * intra
    inter = _dot(q, r0) * qd
    r_ref[...] = r0 * cd + _dot_tn(k.astype(F32) * kd, v)
    yield
    o = _dot(s, v) + inter
    yield
    o = o * lax.rsqrt(jnp.mean(o * o, axis=-1, keepdims=True) + RMS_EPS)
    return (_silu(g.astype(F32)) * o).astype(BF16)


def _ret_chunk_kernel(H, q_ref, k_ref, v_ref, g_ref, intra_ref, qd_ref, kd_ref, cd_ref, o_ref, r_ref):
    @pl.when(pl.program_id(1) == 0)
    def _():
        r_ref[...] = jnp.zeros_like(r_ref)

    dk = q_ref.shape[1] // H
    dv = v_ref.shape[1] // H
    ks = [slice(h * dk, (h + 1) * dk) for h in range(H)]
    vs = [slice(h * dv, (h + 1) * dv) for h in range(H)]
    outs = _run_in_lockstep([
        _ret_head_chunk(q_ref[:, ks[h]], k_ref[:, ks[h]], v_ref[:, vs[h]], g_ref[:, vs[h]],
                        intra_ref[h], qd_ref[h], kd_ref[h], cd_ref[h], r_ref.at[h])
        for h in range(H)])
    for h in range(H):
        o_ref[:, vs[h]] = outs[h]


def _ret_chunk(q, k, v, g, B, T, H):
    N = q.shape[0]
    dk = q.shape[1] // H
    dv = v.shape[1] // H
    C = min(RET_BLOCK, T)
    nc = T // C
    lg = jnp.log(1.0 - 2.0 ** (-5.0 - jnp.arange(H, dtype=F32)))
    n = jnp.arange(C, dtype=F32)
    diff = n[:, None] - n[None, :]
    intra = jnp.where(diff >= 0, jnp.exp(lg[:, None, None] * jnp.maximum(diff, 0.0)), 0.0)
    qd = jnp.broadcast_to(jnp.exp(lg[:, None] * (n[None, :] + 1.0))[:, :, None], (H, C, LANES))
    kd = jnp.broadcast_to(jnp.exp(lg[:, None] * (C - 1.0 - n[None, :]))[:, :, None], (H, C, LANES))
    cd = jnp.broadcast_to(jnp.exp(lg * C)[:, None, None], (H, 8, LANES))
    qd = jnp.concatenate([qd] * (dv // LANES), axis=2)
    kd = jnp.concatenate([kd] * (dk // LANES), axis=2)
    cd = jnp.concatenate([cd] * (dv // LANES), axis=2)[:, :1]
    seq = lambda w: pl.BlockSpec((C, w), lambda b, c: (b * nc + c, 0))
    return pl.pallas_call(
        functools.partial(_ret_chunk_kernel, H),
        grid=(B, nc),
        in_specs=[seq(H * dk), seq(H * dk), seq(H * dv), seq(H * dv)]
        + [_full(t.shape) for t in (intra, qd, kd, cd)],
        out_specs=seq(H * dv),
        out_shape=jax.ShapeDtypeStruct((N, H * dv), BF16),
        scratch_shapes=[pltpu.VMEM((H, dk, dv), F32)],
        compiler_params=_cparams(("parallel", "arbitrary")),
        name="ret_chunk",
    )(q, k, v, g, intra, qd, kd, cd)


def _ret_out_router_kernel(h_ref, o_ref, wo_ref, nw_ref, rt_ref, h_out, u_out, idx_out, gate_out):
    h = h_ref[...] + _dot(o_ref[...], wo_ref[...])
    h_out[...] = h
    u = _rms(h, nw_ref[...])
    u_out[...] = u
    logits = lax.dot_general(rt_ref[...], u, (((1,), (1,)), ((), ())),
                             precision=lax.Precision.HIGHEST, preferred_element_type=F32)
    e_iota = lax.broadcasted_iota(jnp.int32, logits.shape, 0)
    big = jnp.int32(N_EXPERTS)
    m1 = jnp.max(logits, axis=0, keepdims=True)
    i1 = jnp.min(jnp.where(logits == m1, e_iota, big), axis=0, keepdims=True)
    rest = jnp.where(e_iota == i1, -jnp.inf, logits)
    m2 = jnp.max(rest, axis=0, keepdims=True)
    i2 = jnp.min(jnp.where(rest == m2, e_iota, big), axis=0, keepdims=True)
    p2 = jnp.exp(m2 - m1)
    g1 = 1.0 / (1.0 + p2)
    idx_out[...] = jnp.concatenate([i1, i2], axis=0)
    gate_out[...] = jnp.concatenate([g1, p2 * g1], axis=0)


def _ret_out_router(h, o, wo, nw, router_t, tm=512):
    N, D = h.shape
    K = o.shape[1]
    tm = min(tm, N)
    row = lambda w: pl.BlockSpec((tm, w), lambda i: (i, 0))
    col = pl.BlockSpec((TOP_K, tm), lambda i: (0, i))
    return pl.pallas_call(
        _ret_out_router_kernel,
        grid=(N // tm,),
        in_specs=[row(D), row(K), _full(wo.shape), _full(nw.shape), _full(router_t.shape)],
        out_specs=[row(D), row(D), col, col],
        out_shape=[jax.ShapeDtypeStruct((N, D), F32), jax.ShapeDtypeStruct((N, D), F32),
                   jax.ShapeDtypeStruct((TOP_K, N), jnp.int32), jax.ShapeDtypeStruct((TOP_K, N), F32)],
        compiler_params=_cparams(("parallel",)),
        name="ret_out_router",
    )(h, o, wo, nw, router_t)


def _row_copy(src_hbm, dst_ref, sem, s, j):
    return pltpu.make_async_copy(src_hbm.at[pl.ds(s, 1), :], dst_ref.at[pl.ds(j, 1), :], sem)


def _moe_kernel(tmm, q, te_ref, src_ref, dst_ref, u_hbm, w1_ref, w3_ref, w2_ref, y_hbm,
                xbuf, xb_ref, obuf, gsem, ssem):
    i = pl.program_id(0)
    f = pl.program_id(1)
    n_i = pl.num_programs(0)
    n_f = pl.num_programs(1)
    cur = i % 2
    oth = 1 - cur

    def gather(slot, j, tok):
        return _row_copy(u_hbm, xbuf.at[slot], gsem.at[slot], tok, j)

    def scatter(slot, j, row):
        return pltpu.make_async_copy(obuf.at[slot].at[pl.ds(j, 1), :], y_hbm.at[pl.ds(row, 1), :],
                                     ssem.at[slot])

    def wait_rows(make):
        def body(j, c):
            make(j).wait()
            return c
        lax.fori_loop(0, tmm, body, 0, unroll=DMA_UNROLL)

    @pl.when((i == 0) & (f == 0))
    def _():
        obuf[1] = jnp.zeros((tmm, obuf.shape[2]), F32)

        def first(j, c):
            gather(0, j, te_ref[n_i + j]).start()
            return c
        lax.fori_loop(0, tmm, first, 0, unroll=DMA_UNROLL)

    @pl.when(f == 0)
    def _():
        wait_rows(lambda j: gather(cur, j, 0))
        xb_ref[...] = xbuf[cur].astype(BF16)

        @pl.when(i > 0)
        def _():
            wait_rows(lambda j: scatter(cur, j, 0))
        obuf[cur] = jnp.zeros((tmm, obuf.shape[2]), F32)

    def issue_rows(slot, first):
        for j in range(first, first + q):
            gather(slot, j, src_ref[j]).start()
            scatter(slot, j, dst_ref[j]).start()

    for slot in range(2):
        for ff in range(tmm // q):
            pl.when((oth == slot) & (f == ff))(functools.partial(issue_rows, slot, ff * q))

    @pl.when(i < te_ref[n_i + 2 * tmm])
    def _():
        for c in range(w1_ref.shape[2] // MOE_SUB):
            cs = slice(c * MOE_SUB, (c + 1) * MOE_SUB)
            obuf[cur] += _swiglu_chunk(xb_ref[...], w1_ref[0, :, cs], w3_ref[0, :, cs], w2_ref[0, cs, :])

    @pl.when((f == n_f - 1) & (i == n_i - 1))
    def _():
        def last(j, c):
            scatter(cur, j, te_ref[n_i + tmm + j]).start()
            return c
        lax.fori_loop(0, tmm, last, 0, unroll=DMA_UNROLL)
        wait_rows(lambda j: scatter(oth, j, 0))
        wait_rows(lambda j: scatter(cur, j, 0))
        wait_rows(lambda j: gather(oth, j, 0))


def _moe_experts(u, plan, src_next, dst_prev, w1, w3, w2, n_rows_out, tf=1792):
    N, D = u.shape
    Fd = w1.shape[2]
    tmm = MOE_TILE
    n_tiles = src_next.shape[0] // tmm
    nf = Fd // tf
    q = tmm // nf
    smem = lambda: pl.BlockSpec((tmm,), lambda i, f, te: (i,), memory_space=pltpu.SMEM)
    gs = pltpu.PrefetchScalarGridSpec(
        num_scalar_prefetch=1,
        grid=(n_tiles, nf),
        in_specs=[smem(), smem(), pl.BlockSpec(memory_space=pl.ANY),
                  pl.BlockSpec((1, D, tf), lambda i, f, te: (te[i], 0, f)),
                  pl.BlockSpec((1, D, tf), lambda i, f, te: (te[i], 0, f)),
                  pl.BlockSpec((1, tf, D), lambda i, f, te: (te[i], f, 0))],
        out_specs=pl.BlockSpec(memory_space=pl.ANY),
        scratch_shapes=[pltpu.VMEM((2, tmm, D), F32), pltpu.VMEM((tmm, D), BF16),
                        pltpu.VMEM((2, tmm, D), F32), pltpu.SemaphoreType.DMA((2,)),
                        pltpu.SemaphoreType.DMA((2,))],
    )
    return pl.pallas_call(
        functools.partial(_moe_kernel, tmm, q),
        grid_spec=gs,
        out_shape=jax.ShapeDtypeStruct((n_rows_out, D), F32),
        compiler_params=_cparams(("arbitrary", "arbitrary")),
        name="moe_experts",
    )(plan, src_next, dst_prev, u, w1, w3, w2)


def _combine_kernel(h_ref, y0_ref, y1_ref, gate_ref, nw_ref, o_ref):
    gate = gate_ref[...]
    moe = gate[:, 0:1] * y0_ref[...] + gate[:, 1:2] * y1_ref[...]
    o_ref[...] = _rms(h_ref[...] + moe, nw_ref[...])


def _combine(h, y, gate, nw, tc=512):
    N, D = h.shape
    tc = min(tc, N)
    nb = N // tc
    return pl.pallas_call(
        _combine_kernel,
        grid=(nb,),
        in_specs=[pl.BlockSpec((tc, D), lambda i: (i, 0)),
                  pl.BlockSpec((tc, D), lambda i: (i, 0)),
                  pl.BlockSpec((tc, D), lambda i: (nb + i, 0)),
                  pl.BlockSpec((tc, TOP_K), lambda i: (i, 0)),
                  _full(nw.shape)],
        out_specs=pl.BlockSpec((tc, D), lambda i: (i, 0)),
        out_shape=jax.ShapeDtypeStruct((N, D), F32),
        compiler_params=_cparams(("parallel",)),
        name="moe_combine",
    )(h, y, y, gate, nw)


def _route_plan(idx, N):
    tmm = MOE_TILE
    n_asg = TOP_K * N
    n_tiles = n_asg // tmm + N_EXPERTS
    R = n_tiles * tmm
    flat_e = idx.reshape(-1)
    onehot = (flat_e[:, None] == jnp.arange(N_EXPERTS, dtype=jnp.int32)[None, :]).astype(jnp.int32)
    csum = jnp.cumsum(onehot, axis=0)
    rank = jnp.sum(csum * onehot, axis=1) - 1
    counts = csum[-1]
    ptiles = (counts + tmm - 1) // tmm
    tile_end = jnp.cumsum(ptiles)
    tile_start = tile_end - ptiles
    pos = tile_start[flat_e] * tmm + rank
    t = jnp.arange(n_tiles, dtype=jnp.int32)
    tile_e = jnp.sum((t[:, None] >= tile_end[None, :]).astype(jnp.int32), axis=1)
    tile_expert = jnp.minimum(tile_e, N_EXPERTS - 1)
    real_before = jnp.concatenate([jnp.cumsum(counts), jnp.full((1,), n_asg, jnp.int32)])[tile_e]
    spare = n_asg + jnp.arange(R, dtype=jnp.int32) - jnp.repeat(real_before, tmm)
    dst = spare.at[pos].set(jnp.arange(n_asg, dtype=jnp.int32), unique_indices=True,
                            mode="promise_in_bounds")
    src = jnp.where(dst < n_asg, dst % N, 0)
    zeros = jnp.zeros((tmm,), jnp.int32)
    before_first = R + jnp.arange(tmm, dtype=jnp.int32)
    src_next = jnp.concatenate([src[tmm:], zeros])
    dst_prev = jnp.concatenate([before_first, dst[:R - tmm]])
    plan = jnp.concatenate([tile_expert, src[:tmm], dst[R - tmm:], tile_end[-1:]]).astype(jnp.int32)
    return plan, src_next, dst_prev, R + tmm


def _rope_tables(T, dk):
    angle = 1.0 / (ROPE_BASE ** jnp.linspace(0.0, 1.0, dk // 2, dtype=F32))
    angle = jnp.repeat(angle, 2)
    ang = jnp.arange(T, dtype=F32)[:, None] * angle[None, :]
    sign = jnp.where(jnp.arange(dk) % 2 == 0, -1.0, 1.0).astype(F32)
    return jnp.cos(ang), jnp.sin(ang) * sign[None, :]


def kernel(x, norm_mix, norm_ffn, norm_final, a_mu, a_wr, a_wk, a_wv, a_wo, a_w0, a_w1, a_w2, a_a0, a_a1, a_a2, a_g1, a_g2, a_kk, a_ka, a_rk, a_gn_w, a_gn_b, b_wq, b_wk, b_wv, b_wg, b_wo, f_w1, f_w3, f_w2, m_router, m_w1, m_w3, m_w2):
    B, T, D = x.shape
    N = B * T
    bf = lambda w: w.astype(BF16)
    vec = lambda w: w.reshape(1, -1).astype(F32)
    h = x.reshape(N, D)

    r, ld, k, v, a, g = _rwkv_proj(
        h, T, vec(norm_mix[0]), a_mu[0], bf(a_wr[0]), bf(a_wk[0]), bf(a_wv[0]), bf(a_w1[0]), bf(a_w2[0]),
        bf(a_a1[0]), bf(a_a2[0]), bf(a_g1[0]), bf(a_g2[0]), vec(a_w0[0]), vec(a_a0[0]))
    y = _rwkv_recur(r, ld, k, v, a, vec(a_kk[0]), vec(a_ka[0]), vec(a_rk[0]), vec(a_gn_w[0]),
                    vec(a_gn_b[0]), B, T)
    h = _mix_out_dense_ffn(h, y, g, bf(a_wo[0]), vec(norm_ffn[0]), bf(f_w1[0]), bf(f_w3[0]), bf(f_w2[0]))

    dk = b_wq.shape[2] // RET_HEADS
    cos, sin = _rope_tables(T, dk)
    q, kr, vr, gr = _ret_proj(h, T, vec(norm_mix[1]), cos, sin, bf(b_wq[0]), bf(b_wk[0]), bf(b_wv[0]),
                              bf(b_wg[0]))
    o = _ret_chunk(q, kr, vr, gr, B, T, RET_HEADS)
    h, u, idx, gate = _ret_out_router(h, o, bf(b_wo[0]), vec(norm_ffn[1]), m_router[0].T.astype(F32))
    plan, src_next, dst_prev, n_rows = _route_plan(idx, N)
    y = _moe_experts(u, plan, src_next, dst_prev, bf(m_w1[0]), bf(m_w3[0]), bf(m_w2[0]), n_rows)
    out = _combine(h, y, gate.T, vec(norm_final))
    return out.reshape(B, T, D)
```

```python
import functools
import math

import jax
import jax.numpy as jnp
from jax import lax
from jax.experimental import pallas as pl
from jax.experimental.pallas import tpu as pltpu

F32 = jnp.float32
BF16 = jnp.bfloat16

RMS_EPS = 1e-6
RWKV_HEAD = 64
RWKV_GN_EPS = 64e-5
RET_HEADS = 4
ROPE_BASE = 10000.0
N_EXPERTS = 8
TOP_K = 2

LANES = 128
VMEM_LIMIT = 56 * 1024 * 1024

RWKV_CHUNK = 64
RWKV_SUB = 16
RET_BLOCK = 256
MOE_TILE = 512
MOE_SUB = 256
FFN_SUB = 256
DMA_UNROLL = 8


def _cparams(sem):
    return pltpu.CompilerParams(dimension_semantics=sem, vmem_limit_bytes=VMEM_LIMIT)


def _dot(a, b):
    return jnp.dot(a.astype(BF16), b.astype(BF16), preferred_element_type=F32)


def _dot_nt(a, b):
    return lax.dot_general(a.astype(BF16), b.astype(BF16), (((1,), (1,)), ((), ())),
                           preferred_element_type=F32)


def _dot_tn(a, b):
    return lax.dot_general(a.astype(BF16), b.astype(BF16), (((0,), (0,)), ((), ())),
                           preferred_element_type=F32)


def _rms(x, g):
    return x * lax.rsqrt(jnp.mean(x * x, axis=-1, keepdims=True) + RMS_EPS) * g


def _sigmoid(x):
    return 1.0 / (1.0 + jnp.exp(-x))


def _silu(x):
    return x * _sigmoid(x)


def _full(shape):
    n = len(shape)
    return pl.BlockSpec(shape, lambda *_: (0,) * n, pipeline_mode=pl.Buffered(1))


def _rwkv_proj_kernel(T, tm, h_ref, hp_ref, nw_ref, mu_ref, wr_ref, wk_ref, wv_ref, w1_ref, w2_ref,
                      a1_ref, a2_ref, g1_ref, g2_ref, w0_ref, a0_ref,
                      r_out, ld_out, k_out, v_out, a_out, g_out):
    i = pl.program_id(0)
    nw = nw_ref[...]
    u = _rms(h_ref[...], nw)
    up = _rms(hp_ref[...], nw)
    first = (i * tm) % T == 0
    prev_last = jnp.where(first, 0.0, up[7:8, :])
    row = lax.broadcasted_iota(jnp.int32, u.shape, 0)
    us = jnp.where(row == 0, prev_last, pltpu.roll(u, 1, 0))
    dx = us - u

    def mix(j):
        return (u + dx * mu_ref[j:j + 1, :]).astype(BF16)

    r_out[...] = _dot(mix(0), wr_ref[...]).astype(BF16)
    z =w0_ref[...] + _dot(jnp.tanh(_dot(mix(1), w1_ref[...])), w2_ref[...])
    nz = -z
    softplus = jnp.maximum(nz, 0.0) + jnp.log(1.0 + jnp.exp(-jnp.abs(nz)))
    ld_out[...] = -jnp.exp(-softplus - 0.5)
    k_out[...] = _dot(mix(2), wk_ref[...]).astype(BF16)
    v_out[...] = _dot(mix(3), wv_ref[...]).astype(BF16)
    a_out[...] = _sigmoid(a0_ref[...] + _dot(_dot(mix(4), a1_ref[...]), a2_ref[...])).astype(BF16)
    g_out[...] = _dot(_sigmoid(_dot(mix(5), g1_ref[...])), g2_ref[...]).astype(BF16)


def _rwkv_proj(h, T, nw, mu, wr, wk, wv, w1, w2, a1, a2, g1, g2, w0, a0, tm=512):
    N, D = h.shape
    tm = min(tm, T)
    row = pl.BlockSpec((tm, D), lambda i: (i, 0))
    prev = pl.BlockSpec((8, D), lambda i: (jnp.maximum(i * (tm // 8) - 1, 0), 0))
    ws = [nw, mu, wr, wk, wv, w1, w2, a1, a2, g1, g2, w0, a0]
    out = lambda dt: jax.ShapeDtypeStruct((N, D), dt)
    return pl.pallas_call(
        functools.partial(_rwkv_proj_kernel, T, tm),
        grid=(N // tm,),
        in_specs=[row, prev] + [_full(w.shape) for w in ws],
        out_specs=[row] * 6,
        out_shape=[out(BF16), out(F32), out(BF16), out(BF16), out(BF16), out(BF16)],
        compiler_params=_cparams(("parallel",)),
        name="rwkv_proj",
    )(h, h, *ws)


def _rwkv_pair_chunk(r, ld, cum, k, v, a, kkp, kap, rkp, gnw, gnb, s_ref):
    L = r.shape[0]
    L2 = 2 * L
    lane = lax.broadcasted_iota(jnp.int32, (L, LANES), 1)
    m0 = lane < RWKV_HEAD

    def seg_sum(x):
        s0 = jnp.sum(jnp.where(m0, x, 0.0), axis=-1, keepdims=True)
        s1 = jnp.sum(jnp.where(m0, 0.0, x), axis=-1, keepdims=True)
        return jnp.where(m0, s0, s1)

    def stack(x):
        return jnp.concatenate([jnp.where(m0, x, 0.0), jnp.where(m0, 0.0, x)], axis=0)

    kk = k * kkp
    kk = kk / jnp.maximum(jnp.sqrt(seg_sum(kk * kk)), 1e-12)
    kmod = k * (1.0 + (a - 1.0) * kap)
    aa = -kk
    bb = kk * a

    tot = cum[L - 1:L, :]
    e_cum = jnp.exp(cum)
    e_inv = jnp.exp(-cum)
    e_fin = jnp.exp(tot - cum)
    a_t = stack(aa * jnp.exp(cum - ld))
    r_t = stack(r * e_cum)
    b_t = stack(bb * e_inv)
    k_t = stack(kmod * e_inv)
    b_h = stack(bb * e_fin)
    k_h = stack(kmod * e_fin)
    v_s = stack(v)

    mm = _dot_nt(jnp.concatenate([a_t, r_t], axis=0), jnp.concatenate([b_t, k_t], axis=0))
    yield
    si =lax.broadcasted_iota(jnp.int32, (L2, L2), 0)
    sj = lax.broadcasted_iota(jnp.int32, (L2, L2), 1)
    low = sj < si
    lowi = sj <= si
    m_ab = jnp.where(low, mm[:L2, :L2], 0.0)
    m_ak = jnp.where(low, mm[:L2, L2:], 0.0)
    m_rb = jnp.where(lowi, mm[L2:, :L2], 0.0)
    m_rk = jnp.where(lowi, mm[L2:, L2:], 0.0)

    eye = jnp.where(si == sj, 1.0, 0.0).astype(F32)
    sub_shift = RWKV_SUB.bit_length() - 1
    same = (si >> sub_shift) == (sj >> sub_shift)
    dg = jnp.where(same, m_ab, 0.0)
    og = jnp.where(same, 0.0, m_ab)
    d2 = _dot(dg, dg)
    s0 = s_ref[...]
    ar_s = _dot_nt(jnp.concatenate([a_t, r_t], axis=0), s0)
    c = ar_s[:L2] + _dot(m_ak, v_s)
    yield
    d4 = _dot(d2, d2)
    x1 = eye + dg + d2 + _dot(dg, d2)
    yield
    d8 = _dot(d4, d4)
    yield
    x2 = eye + d4 + d8 + _dot(d4, d8)
    yield
    t_d = _dot(x1, x2)
    yield
    e1 = _dot(t_d, og)
    yield
    e2 = _dot(e1, e1)
    yield
    t_f = eye + e1 + e2 + _dot(e1, e2)
    yield
    t_m = _dot(t_f, t_d)
    yield
    u = _dot(t_m, c)
    yield
    uv = jnp.concatenate([u, v_s], axis=0)
    ys = ar_s[L2:] + _dot(jnp.concatenate([m_rb, m_rk], axis=1), uv)
    s_ref[...] = s0 * jnp.exp(tot) + _dot_tn(uv, jnp.concatenate([b_h, k_h], axis=0))
    yield
    y = ys[:L] + ys[L:]

    inv_n = 1.0 / RWKV_HEAD
    mean = seg_sum(y) * inv_n
    yc = y - mean
    var = seg_sum(yc * yc) * inv_n
    yn = yc * lax.rsqrt(var + RWKV_GN_EPS) * gnw + gnb
    return yn + seg_sum(r * kmod * rkp) * v


def _run_in_lockstep(gens):
    results = [None] * len(gens)
    live = list(range(len(gens)))
    while live:
        still = []
        for i in live:
            try:
                next(gens[i])
                still.append(i)
            except StopIteration as done:
                results[i] = done.value
        live = still
    return results


def _rwkv_recur_kernel(G, L, r_ref, ld_ref, k_ref, v_ref, a_ref, kk_ref, ka_ref, rk_ref, gw_ref, gb_ref,
                       y_ref, s_ref):
    @pl.when(pl.program_id(0) == 0)
    def _():
        s_ref[...] = jnp.zeros_like(s_ref)

    B = r_ref.shape[0]
    blocks = [(b, slice(g * LANES, (g + 1) * LANES)) for b in range(B) for g in range(G)]
    ti = lax.broadcasted_iota(jnp.int32, (L, L), 0)
    tj = lax.broadcasted_iota(jnp.int32, (L, L), 1)
    tri = jnp.where(tj <= ti, 1.0, 0.0).astype(BF16)

    def chunk(c, carry):
        rows = pl.ds(pl.multiple_of(c * L, L), L)
        lds, cums = [], []
        for b in range(B):
            ld = ld_ref[b, rows, :]
            cum = jnp.zeros_like(ld)
            rest = ld
            for _ in range(3):
                part = rest.astype(BF16)
                cum = cum + jnp.dot(tri, part, preferred_element_type=F32)
                rest = rest - part.astype(F32)
            lds.append(ld)
            cums.append(cum)
        f32 = lambda ref, b, sl: ref[b, rows, sl].astype(F32)
        ys = _run_in_lockstep([
            _rwkv_pair_chunk(f32(r_ref, b, sl), lds[b][:, sl], cums[b][:, sl], f32(k_ref, b, sl),
                             f32(v_ref, b, sl), f32(a_ref, b, sl), kk_ref[:, sl], ka_ref[:, sl],
                             rk_ref[:, sl], gw_ref[:, sl], gb_ref[:, sl], s_ref.at[n])
            for n, (b, sl) in enumerate(blocks)])
        for (b, sl), y in zip(blocks, ys):
            y_ref[b, rows, sl] = y.astype(BF16)
        return carry

    lax.fori_loop(0, r_ref.shape[1] // L, chunk, 0)


def _rwkv_recur(r, ld, k, v, a, kkp, kap, rkp, gnw, gnb, B, T, chunks_per_step=4):
    N, D = r.shape
    L = RWKV_CHUNK
    G = D // LANES
    rows = L * chunks_per_step
    seq = pl.BlockSpec((B, rows, D), lambda c: (0, c, 0))
    par = pl.BlockSpec((1, D), lambda c: (0, 0))
    b3 = lambda t: t.reshape(B, T, D)
    y = pl.pallas_call(
        functools.partial(_rwkv_recur_kernel, G, L),
        grid=(T // rows,),
        in_specs=[seq] * 5 + [par] * 5,
        out_specs=seq,
        out_shape=jax.ShapeDtypeStruct((B, T, D), BF16),
        scratch_shapes=[pltpu.VMEM((B * G, LANES, LANES), F32)],
        compiler_params=_cparams(("arbitrary",)),
        name="rwkv_recur",
    )(b3(r), b3(ld), b3(k), b3(v), b3(a), kkp, kap, rkp, gnw, gnb)
    return y.reshape(N, D)


def _swiglu_chunk(u, w1, w3, w2):
    return _dot(_silu(_dot(u, w1)) * _dot(u, w3), w2)


def _dense_ffn_kernel(h_ref, y_ref, g_ref, wo_ref, nw_ref, w1_ref, w3_ref, w2_ref, o_ref, u_ref):
    h = h_ref[...] + _dot(y_ref[...].astype(F32) * g_ref[...].astype(F32), wo_ref[...])
    u_ref[...] = _rms(h, nw_ref[...]).astype(BF16)
    o_ref[...] = h
    for c in range(w1_ref.shape[1] // FFN_SUB):
        cs = slice(c * FFN_SUB, (c + 1) * FFN_SUB)
        o_ref[...] += _swiglu_chunk(u_ref[...], w1_ref[:, cs], w3_ref[:, cs], w2_ref[cs, :])


def _mix_out_dense_ffn(h, y, g, wo, nw, w1, w3, w2, tm=512):
    N, D = h.shape
    tm = min(tm, N)
    row = pl.BlockSpec((tm, D), lambda i: (i, 0))
    return pl.pallas_call(
        _dense_ffn_kernel,
        grid=(N // tm,),
        in_specs=[row, row, row] + [_full(w.shape) for w in (wo, nw, w1, w3, w2)],
        out_specs=row,
        out_shape=jax.ShapeDtypeStruct((N, D), F32),
        scratch_shapes=[pltpu.VMEM((tm, D), BF16)],
        compiler_params=_cparams(("parallel",)),
        name="dense_ffn",
    )(h, y, g, wo, nw, w1, w3, w2)


def _ret_proj_kernel(dk, h_ref, nw_ref, cos_ref, sin_ref, wq_ref, wk_ref, wv_ref, wg_ref,
                     q_out, k_out, v_out, g_out):
    u = _rms(h_ref[...], nw_ref[...]).astype(BF16)
    nh = wq_ref.shape[1] // dk
    cos = jnp.concatenate([cos_ref[...]] * nh, axis=1)
    sin = jnp.concatenate([sin_ref[...]] * nh, axis=1)
    lane = lax.broadcasted_iota(jnp.int32, cos.shape, 1)
    even = (lane & 1) == 0
    width = cos.shape[1]

    def rope(t):
        nxt = pltpu.roll(t, width - 1, 1)
        prv = pltpu.roll(t, 1, 1)
        return t * cos + jnp.where(even, nxt, prv) * sin

    q_out[...] = rope(_dot(u, wq_ref[...])).astype(BF16)
    k_out[...] = rope(_dot(u, wk_ref[...]) * (dk ** -0.5)).astype(BF16)
    v_out[...] = _dot(u, wv_ref[...]).astype(BF16)
    g_out[...] = _dot(u, wg_ref[...]).astype(BF16)


def _ret_proj(h, T, nw, cos, sin, wq, wk, wv, wg, tm=512):
    N, D = h.shape
    dk = cos.shape[1]
    tm = min(tm, T)
    tpb = T // tm
    row = lambda w: pl.BlockSpec((tm, w), lambda i: (i, 0))
    tab = pl.BlockSpec((tm, dk), lambda i: (i % tpb, 0))
    return pl.pallas_call(
        functools.partial(_ret_proj_kernel, dk),
        grid=(N // tm,),
        in_specs=[row(D), _full(nw.shape), tab, tab] + [_full(w.shape) for w in (wq, wk, wv, wg)],
        out_specs=[row(wq.shape[1]), row(wk.shape[1]), row(wv.shape[1]), row(wg.shape[1])],
        out_shape=[jax.ShapeDtypeStruct((N, w.shape[1]), BF16) for w in (wq, wk, wv, wg)],
        compiler_params=_cparams(("parallel",)),
        name="ret_proj",
    )(h, nw, cos, sin, wq, wk, wv, wg)


def _ret_head_chunk(q, k, v, g, intra, qd, kd, cd, r_ref):
    r0 = r_ref[...]
    s = _dot_nt(q, k) * intra
    inter = _dot(q, r0) * qd
    r_ref[...] = r0 * cd + _dot_tn(k.astype(F32) * kd, v)
    yield
    o = _dot(s, v) + inter
    yield
    o = o * lax.rsqrt(jnp.mean(o * o, axis=-1, keepdims=True) + RMS_EPS)
    return (_silu(g.astype(F32)) * o).astype(BF16)


def _ret_chunk_kernel(H, q_ref, k_ref, v_ref, g_ref, intra_ref, qd_ref, kd_ref, cd_ref, o_ref, r_ref):
    @pl.when(pl.program_id(1) == 0)
    def _():
        r_ref[...] = jnp.zeros_like(r_ref)

    dk = q_ref.shape[1] // H
    dv = v_ref.shape[1] // H
    ks = [slice(h * dk, (h + 1) * dk) for h in range(H)]
    vs = [slice(h * dv, (h + 1) * dv) for h in range(H)]
    outs = _run_in_lockstep([
        _ret_head_chunk(q_ref[:, ks[h]], k_ref[:, ks[h]], v_ref[:, vs[h]], g_ref[:, vs[h]],
                        intra_ref[h], qd_ref[h], kd_ref[h], cd_ref[h], r_ref.at[h])
        for h in range(H)])
    for h in range(H):
        o_ref[:, vs[h]] = outs[h]


def _ret_chunk(q, k, v, g, B, T, H):
    N = q.shape[0]
    dk = q.shape[1] // H
    dv = v.shape[1] // H
    C = min(RET_BLOCK, T)
    nc = T // C
    lg = jnp.log(1.0 - 2.0 ** (-5.0 - jnp.arange(H, dtype=F32)))
    n = jnp.arange(C, dtype=F32)
    diff = n[:, None] - n[None, :]
    intra = jnp.where(diff >= 0, jnp.exp(lg[:, None, None] * jnp.maximum(diff, 0.0)), 0.0)
    qd = jnp.broadcast_to(jnp.exp(lg[:, None] * (n[None, :] + 1.0))[:, :, None], (H, C, LANES))
    kd = jnp.broadcast_to(jnp.exp(lg[:, None] * (C - 1.0 - n[None, :]))[:, :, None], (H, C, LANES))
    cd = jnp.broadcast_to(jnp.exp(lg * C)[:, None, None], (H, 8, LANES))
    qd = jnp.concatenate([qd] * (dv // LANES), axis=2)
    kd = jnp.concatenate([kd] * (dk // LANES), axis=2)
    cd = jnp.concatenate([cd] * (dv // LANES), axis=2)[:, :1]
    seq = lambda w: pl.BlockSpec((C, w), lambda b, c: (b * nc + c, 0))
    return pl.pallas_call(
        functools.partial(_ret_chunk_kernel, H),
        grid=(B, nc),
        in_specs=[seq(H * dk), seq(H * dk), seq(H * dv), seq(H * dv)]
        + [_full(t.shape) for t in (intra, qd, kd, cd)],
        out_specs=seq(H * dv),
        out_shape=jax.ShapeDtypeStruct((N, H * dv), BF16),
        scratch_shapes=[pltpu.VMEM((H, dk, dv), F32)],
        compiler_params=_cparams(("parallel", "arbitrary")),
        name="ret_chunk",
    )(q, k, v, g, intra, qd, kd, cd)


def _ret_out_router_kernel(h_ref, o_ref, wo_ref, nw_ref, rt_ref, h_out, u_out, idx_out, gate_out):
    h = h_ref[...] + _dot(o_ref[...], wo_ref[...])
    h_out[...] = h
    u = _rms(h, nw_ref[...])
    u_out[...] = u
    logits = lax.dot_general(rt_ref[...], u, (((1,), (1,)), ((), ())),
                             precision=lax.Precision.HIGHEST, preferred_element_type=F32)
    e_iota = lax.broadcasted_iota(jnp.int32, logits.shape, 0)
    big = jnp.int32(N_EXPERTS)
    m1 = jnp.max(logits, axis=0, keepdims=True)
    i1 = jnp.min(jnp.where(logits == m1, e_iota, big), axis=0, keepdims=True)
    rest = jnp.where(e_iota == i1, -jnp.inf, logits)
    m2 = jnp.max(rest, axis=0, keepdims=True)
    i2 = jnp.min(jnp.where(rest == m2, e_iota, big), axis=0, keepdims=True)
    p2 = jnp.exp(m2 - m1)
    g1 = 1.0 / (1.0 + p2)
    idx_out[...] = jnp.concatenate([i1, i2], axis=0)
    gate_out[...] = jnp.concatenate([g1, p2 * g1], axis=0)


def _ret_out_router(h, o, wo, nw, router_t, tm=512):
    N, D = h.shape
    K = o.shape[1]
    tm = min(tm, N)
    row = lambda w: pl.BlockSpec((tm, w), lambda i: (i, 0))
    col = pl.BlockSpec((TOP_K, tm), lambda i: (0, i))
    return pl.pallas_call(
        _ret_out_router_kernel,
        grid=(N // tm,),
        in_specs=[row(D), row(K), _full(wo.shape), _full(nw.shape), _full(router_t.shape)],
        out_specs=[row(D), row(D), col, col],
        out_shape=[jax.ShapeDtypeStruct((N, D), F32), jax.ShapeDtypeStruct((N, D), F32),
                   jax.ShapeDtypeStruct((TOP_K, N), jnp.int32), jax.ShapeDtypeStruct((TOP_K, N), F32)],
        compiler_params=_cparams(("parallel",)),
        name="ret_out_router",
    )(h, o, wo, nw, router_t)


def _row_copy(src_hbm, dst_ref, sem, s, j):
    return pltpu.make_async_copy(src_hbm.at[pl.ds(s, 1), :], dst_ref.at[pl.ds(j, 1), :], sem)


def _moe_kernel(tmm, q, te_ref, src_ref, dst_ref, u_hbm, w1_ref, w3_ref, w2_ref, y_hbm,
                xbuf, xb_ref, obuf, gsem, ssem):
    i = pl.program_id(0)
    f = pl.program_id(1)
    n_i = pl.num_programs(0)
    n_f = pl.num_programs(1)
    cur = i % 2
    oth = 1 - cur

    def gather(slot, j, tok):
        return _row_copy(u_hbm, xbuf.at[slot], gsem.at[slot], tok, j)

    def scatter(slot, j, row):
        return pltpu.make_async_copy(obuf.at[slot].at[pl.ds(j, 1), :], y_hbm.at[pl.ds(row, 1), :],
                                     ssem.at[slot])

    def wait_rows(make):
        def body(j, c):
            make(j).wait()
            return c
        lax.fori_loop(0, tmm, body, 0, unroll=DMA_UNROLL)

    @pl.when((i == 0) & (f == 0))
    def _():
        obuf[1] = jnp.zeros((tmm, obuf.shape[2]), F32)

        def first(j, c):
            gather(0, j, te_ref[n_i + j]).start()
            return c
        lax.fori_loop(0, tmm, first, 0, unroll=DMA_UNROLL)

    @pl.when(f == 0)
    def _():
        wait_rows(lambda j: gather(cur, j, 0))
        xb_ref[...] = xbuf[cur].astype(BF16)

        @pl.when(i > 0)
        def _():
            wait_rows(lambda j: scatter(cur, j, 0))
        obuf[cur] = jnp.zeros((tmm, obuf.shape[2]), F32)

    def issue_rows(slot, first):
        for j in range(first, first + q):
            gather(slot, j, src_ref[j]).start()
            scatter(slot, j, dst_ref[j]).start()

    for slot in range(2):
        for ff in range(tmm // q):
            pl.when((oth == slot) & (f == ff))(functools.partial(issue_rows, slot, ff * q))

    @pl.when(i < te_ref[n_i + 2 * tmm])
    def _():
        for c in range(w1_ref.shape[2] // MOE_SUB):
            cs = slice(c * MOE_SUB, (c + 1) * MOE_SUB)
            obuf[cur] += _swiglu_chunk(xb_ref[...], w1_ref[0, :, cs], w3_ref[0, :, cs], w2_ref[0, cs, :])

    @pl.when((f == n_f - 1) & (i == n_i - 1))
    def _():
        def last(j, c):
            scatter(cur, j, te_ref[n_i + tmm + j]).start()
            return c
        lax.fori_loop(0, tmm, last, 0, unroll=DMA_UNROLL)
        wait_rows(lambda j: scatter(oth, j, 0))
        wait_rows(lambda j: scatter(cur, j, 0))
        wait_rows(lambda j: gather(oth, j, 0))


def _moe_experts(u, plan, src_next, dst_prev, w1, w3, w2, n_rows_out, tf=1792):
    N, D = u.shape
    Fd = w1.shape[2]
    tmm = MOE_TILE
    n_tiles = src_next.shape[0] // tmm
    nf = Fd // tf
    q = tmm // nf
    smem = lambda: pl.BlockSpec((tmm,), lambda i, f, te: (i,), memory_space=pltpu.SMEM)
    gs = pltpu.PrefetchScalarGridSpec(
        num_scalar_prefetch=1,
        grid=(n_tiles, nf),
        in_specs=[smem(), smem(), pl.BlockSpec(memory_space=pl.ANY),
                  pl.BlockSpec((1, D, tf), lambda i, f, te: (te[i], 0, f)),
                  pl.BlockSpec((1, D, tf), lambda i, f, te: (te[i], 0, f)),
                  pl.BlockSpec((1, tf, D), lambda i, f, te: (te[i], f, 0))],
        out_specs=pl.BlockSpec(memory_space=pl.ANY),
        scratch_shapes=[pltpu.VMEM((2, tmm, D), F32), pltpu.VMEM((tmm, D), BF16),
                        pltpu.VMEM((2, tmm, D), F32), pltpu.SemaphoreType.DMA((2,)),
                        pltpu.SemaphoreType.DMA((2,))],
    )
    return pl.pallas_call(
        functools.partial(_moe_kernel, tmm, q),
        grid_spec=gs,
        out_shape=jax.ShapeDtypeStruct((n_rows_out, D), F32),
        compiler_params=_cparams(("arbitrary", "arbitrary")),
        name="moe_experts",
    )(plan, src_next, dst_prev, u, w1, w3, w2)


def _combine_kernel(h_ref, y0_ref, y1_ref, gate_ref, nw_ref, o_ref):
    gate = gate_ref[...]
    moe = gate[:, 0:1] * y0_ref[...] + gate[:, 1:2] * y1_ref[...]
    o_ref[...] = _rms(h_ref[...] + moe, nw_ref[...])


def _combine(h, y, gate, nw, tc=512):
    N, D = h.shape
    tc = min(tc, N)
    nb = N // tc
    return pl.pallas_call(
        _combine_kernel,
        grid=(nb,),
        in_specs=[pl.BlockSpec((tc, D), lambda i: (i, 0)),
                  pl.BlockSpec((tc, D), lambda i: (i, 0)),
                  pl.BlockSpec((tc, D), lambda i: (nb + i, 0)),
                  pl.BlockSpec((tc, TOP_K), lambda i: (i, 0)),
                  _full(nw.shape)],
        out_specs=pl.BlockSpec((tc, D), lambda i: (i, 0)),
        out_shape=jax.ShapeDtypeStruct((N, D), F32),
        compiler_params=_cparams(("parallel",)),
        name="moe_combine",
    )(h, y, y, gate, nw)


def _route_plan(idx, N):
    tmm = MOE_TILE
    n_asg = TOP_K * N
    n_tiles = n_asg // tmm + N_EXPERTS
    R = n_tiles * tmm
    flat_e = idx.reshape(-1)
    asg = jnp.arange(n_asg, dtype=jnp.int32)
    order = jnp.sort(flat_e * n_asg + asg) % n_asg
    counts = jnp.sum((flat_e[:, None] == jnp.arange(N_EXPERTS, dtype=jnp.int32)[None, :]).astype(jnp.int32),
                     axis=0)
    ptiles = (counts + tmm - 1) // tmm
    tile_end = jnp.cumsum(ptiles)
    tile_start = tile_end - ptiles
    cum_incl = jnp.cumsum(counts)
    t = jnp.arange(n_tiles, dtype=jnp.int32)
    tile_e = jnp.sum((t[:, None] >= tile_end[None, :]).astype(jnp.int32), axis=1)
    tile_expert = jnp.minimum(tile_e, N_EXPERTS - 1)
    k = t - tile_start[tile_expert]
    first = jnp.minimum((cum_incl - counts)[tile_expert] + k * tmm, n_asg)
    n_real = jnp.where(tile_e < N_EXPERTS, jnp.clip(counts[tile_expert] - k * tmm, 0, tmm), 0)
    order_pad = jnp.concatenate([order, jnp.zeros((tmm,), jnp.int32)])
    rows = jax.vmap(lambda s: lax.dynamic_slice(order_pad, (s,), (tmm,)))(first)
    j = jnp.arange(tmm, dtype=jnp.int32)
    real_before = jnp.concatenate([cum_incl, jnp.full((1,), n_asg, jnp.int32)])[tile_e]
    spare = n_asg + (t * tmm - real_before)[:, None] + j[None, :]
    dst = jnp.where(j[None, :] < n_real[:, None], rows, spare).reshape(R)
    src = jnp.where(dst < n_asg, dst % N, 0)
    zeros = jnp.zeros((tmm,), jnp.int32)
    before_first = R + jnp.arange(tmm, dtype=jnp.int32)
    src_next = jnp.concatenate([src[tmm:], zeros])
    dst_prev = jnp.concatenate([before_first, dst[:R - tmm]])
    plan = jnp.concatenate([tile_expert, src[:tmm], dst[R - tmm:], tile_end[-1:]]).astype(jnp.int32)
    return plan, src_next, dst_prev, R + tmm


def _rope_tables(T, dk):
    angle = 1.0 / (ROPE_BASE ** jnp.linspace(0.0, 1.0, dk // 2, dtype=F32))
    angle = jnp.repeat(angle, 2)
    ang = jnp.arange(T, dtype=F32)[:, None] * angle[None, :]
    sign = jnp.where(jnp.arange(dk) % 2 == 0, -1.0, 1.0).astype(F32)
    return jnp.cos(ang), jnp.sin(ang) * sign[None, :]


def kernel(x, norm_mix, norm_ffn, norm_final, a_mu, a_wr, a_wk, a_wv, a_wo, a_w0, a_w1, a_w2, a_a0, a_a1, a_a2, a_g1, a_g2, a_kk, a_ka, a_rk, a_gn_w, a_gn_b, b_wq, b_wk, b_wv, b_wg, b_wo, f_w1, f_w3, f_w2, m_router, m_w1, m_w3, m_w2):
    B, T, D = x.shape
    N = B * T
    bf = lambda w: w.astype(BF16)
    vec = lambda w: w.reshape(1, -1).astype(F32)
    h = x.reshape(N, D)

    r, ld, k, v, a, g = _rwkv_proj(
        h, T, vec(norm_mix[0]), a_mu[0], bf(a_wr[0]), bf(a_wk[0]), bf(a_wv[0]), bf(a_w1[0]), bf(a_w2[0]),
        bf(a_a1[0]), bf(a_a2[0]), bf(a_g1[0]), bf(a_g2[0]), vec(a_w0[0]), vec(a_a0[0]))
    y = _rwkv_recur(r, ld, k, v, a, vec(a_kk[0]), vec(a_ka[0]), vec(a_rk[0]), vec(a_gn_w[0]),
                    vec(a_gn_b[0]), B, T)
    h = _mix_out_dense_ffn(h, y, g, bf(a_wo[0]), vec(norm_ffn[0]), bf(f_w1[0]), bf(f_w3[0]), bf(f_w2[0]))

    dk = b_wq.shape[2] // RET_HEADS
    cos, sin = _rope_tables(T, dk)
    q, kr, vr, gr = _ret_proj(h, T, vec(norm_mix[1]), cos, sin, bf(b_wq[0]), bf(b_wk[0]), bf(b_wv[0]),
                              bf(b_wg[0]))
    o = _ret_chunk(q, kr, vr, gr, B, T, RET_HEADS)
    h, u, idx, gate = _ret_out_router(h, o, bf(b_wo[0]), vec(norm_ffn[1]), m_router[0].T.astype(F32))
    plan, src_next, dst_prev, n_rows = _route_plan(idx, N)
    y = _moe_experts(u, plan, src_next, dst_prev, bf(m_w1[0]), bf(m_w3[0]), bf(m_w2[0]), n_rows)
    out = _combine(h, y, gate.T, vec(norm_final))
    return out.reshape(B, T, D)
```

```python
import functools
import math

import jax
import jax.numpy as jnp
from jax import lax
from jax.experimental import pallas as pl
from jax.experimental.pallas import tpu as pltpu

F32 = jnp.float32
BF16 = jnp.bfloat16

RMS_EPS = 1e-6
RWKV_HEAD = 64
RWKV_GN_EPS = 64e-5
RET_HEADS = 4
ROPE_BASE = 10000.0
N_EXPERTS = 8
TOP_K = 2

LANES = 128
VMEM_LIMIT = 56 * 1024 * 1024

RWKV_CHUNK = 64
RWKV_SUB = 16
RET_BLOCK = 256
MOE_TILE = 512
MOE_SUB = 256
FFN_SUB = 256
DMA_UNROLL = 8


def _cparams(sem):
    return pltpu.CompilerParams(dimension_semantics=sem, vmem_limit_bytes=VMEM_LIMIT)


def _dot(a, b):
    return jnp.dot(a.astype(BF16), b.astype(BF16), preferred_element_type=F32)


def _dot_nt(a, b):
    return lax.dot_general(a.astype(BF16), b.astype(BF16), (((1,), (1,)), ((), ())),
                           preferred_element_type=F32)


def _dot_tn(a, b):
    return lax.dot_general(a.astype(BF16), b.astype(BF16), (((0,), (0,)), ((), ())),
                           preferred_element_type=F32)


def _rms(x, g):
    return x * lax.rsqrt(jnp.mean(x * x, axis=-1, keepdims=True) + RMS_EPS) * g


def _sigmoid(x):
    return 1.0 / (1.0 + jnp.exp(-x))


def _silu(x):
    return x * _sigmoid(x)


def _full(shape):
    n = len(shape)
    return pl.BlockSpec(shape, lambda *_: (0,) * n, pipeline_mode=pl.Buffered(1))


def _rwkv_proj_kernel(T, tm, h_ref, hp_ref, nw_ref, mu_ref, wr_ref, wk_ref, wv_ref, w1_ref, w2_ref,
                      a1_ref, a2_ref, g1_ref, g2_ref, w0_ref, a0_ref,
                      r_out, ld_out, k_out, v_out, a_out, g_out):
    i = pl.program_id(0)
    nw = nw_ref[...]
    u = _rms(h_ref[...], nw)
    up = _rms(hp_ref[...], nw)
    first = (i * tm) % T == 0
    prev_last = jnp.where(first, 0.0, up[7:8, :])
    row = lax.broadcasted_iota(jnp.int32, u.shape, 0)
    us = jnp.where(row == 0, prev_last, pltpu.roll(u, 1, 0))
    dx = us - u

    def mix(j):
        return (u + dx * mu_ref[j:j + 1, :]).astype(BF16)

    r_out[...] = _dot(mix(0), wr_ref[...]).astype(BF16)
    z =w0_ref[...] + _dot(jnp.tanh(_dot(mix(1), w1_ref[...])), w2_ref[...])
    nz = -z
    softplus = jnp.maximum(nz, 0.0) + jnp.log(1.0 + jnp.exp(-jnp.abs(nz)))
    ld_out[...] = -jnp.exp(-softplus - 0.5)
    k_out[...] = _dot(mix(2), wk_ref[...]).astype(BF16)
    v_out[...] = _dot(mix(3), wv_ref[...]).astype(BF16)
    a_out[...] = _sigmoid(a0_ref[...] + _dot(_dot(mix(4), a1_ref[...]), a2_ref[...])).astype(BF16)
    g_out[...] = _dot(_sigmoid(_dot(mix(5), g1_ref[...])), g2_ref[...]).astype(BF16)


def _rwkv_proj(h, T, nw, mu, wr, wk, wv, w1, w2, a1, a2, g1, g2, w0, a0, tm=512):
    N, D = h.shape
    tm = min(tm, T)
    row = pl.BlockSpec((tm, D), lambda i: (i, 0))
    prev = pl.BlockSpec((8, D), lambda i: (jnp.maximum(i * (tm // 8) - 1, 0), 0))
    ws = [nw, mu, wr, wk, wv, w1, w2, a1, a2, g1, g2, w0, a0]
    out = lambda dt: jax.ShapeDtypeStruct((N, D), dt)
    return pl.pallas_call(
        functools.partial(_rwkv_proj_kernel, T, tm),
        grid=(N // tm,),
        in_specs=[row, prev] + [_full(w.shape) for w in ws],
        out_specs=[row] * 6,
        out_shape=[out(BF16), out(F32), out(BF16), out(BF16), out(BF16), out(BF16)],
        compiler_params=_cparams(("parallel",)),
        name="rwkv_proj",
    )(h, h, *ws)


def _rwkv_pair_chunk(r, ld, cum, k, v, a, kkp, kap, rkp, gnw, gnb, s_ref):
    L = r.shape[0]
    L2 = 2 * L
    lane = lax.broadcasted_iota(jnp.int32, (L, LANES), 1)
    m0 = lane < RWKV_HEAD

    def seg_sum(x):
        s0 = jnp.sum(jnp.where(m0, x, 0.0), axis=-1, keepdims=True)
        s1 = jnp.sum(jnp.where(m0, 0.0, x), axis=-1, keepdims=True)
        return jnp.where(m0, s0, s1)

    def stack(x):
        return jnp.concatenate([jnp.where(m0, x, 0.0), jnp.where(m0, 0.0, x)], axis=0)

    kk = k * kkp
    kk = kk / jnp.maximum(jnp.sqrt(seg_sum(kk * kk)), 1e-12)
    kmod = k * (1.0 + (a - 1.0) * kap)
    aa = -kk
    bb = kk * a

    tot = cum[L - 1:L, :]
    e_cum = jnp.exp(cum)
    e_inv = jnp.exp(-cum)
    e_fin = jnp.exp(tot - cum)
    a_t = stack(aa * jnp.exp(cum - ld))
    r_t = stack(r * e_cum)
    b_t = stack(bb * e_inv)
    k_t = stack(kmod * e_inv)
    b_h = stack(bb * e_fin)
    k_h = stack(kmod * e_fin)
    v_s = stack(v)

    mm = _dot_nt(jnp.concatenate([a_t, r_t], axis=0), jnp.concatenate([b_t, k_t], axis=0))
    yield
    si =lax.broadcasted_iota(jnp.int32, (L2, L2), 0)
    sj = lax.broadcasted_iota(jnp.int32, (L2, L2), 1)
    low = sj < si
    lowi = sj <= si
    m_ab = jnp.where(low, mm[:L2, :L2], 0.0)
    m_ak = jnp.where(low, mm[:L2, L2:], 0.0)
    m_rb = jnp.where(lowi, mm[L2:, :L2], 0.0)
    m_rk = jnp.where(lowi, mm[L2:, L2:], 0.0)

    eye = jnp.where(si == sj, 1.0, 0.0).astype(F32)
    sub_shift = RWKV_SUB.bit_length() - 1
    same = (si >> sub_shift) == (sj >> sub_shift)
    dg = jnp.where(same, m_ab, 0.0)
    og = jnp.where(same, 0.0, m_ab)
    d2 = _dot(dg, dg)
    s0 = s_ref[...]
    ar_s = _dot_nt(jnp.concatenate([a_t, r_t], axis=0), s0)
    c = ar_s[:L2] + _dot(m_ak, v_s)
    yield
    d4 = _dot(d2, d2)
    x1 = eye + dg + d2 + _dot(dg, d2)
    yield
    d8 = _dot(d4, d4)
    yield
    x2 = eye + d4 + d8 + _dot(d4, d8)
    yield
    t_d = _dot(x1, x2)
    yield
    e1 = _dot(t_d, og)
    yield
    e2 = _dot(e1, e1)
    yield
    t_f = eye + e1 + e2 + _dot(e1, e2)
    yield
    t_m = _dot(t_f, t_d)
    yield
    u = _dot(t_m, c)
    yield
    uv = jnp.concatenate([u, v_s], axis=0)
    ys = ar_s[L2:] + _dot(jnp.concatenate([m_rb, m_rk], axis=1), uv)
    s_ref[...] = s0 * jnp.exp(tot) + _dot_tn(uv, jnp.concatenate([b_h, k_h], axis=0))
    yield
    y = ys[:L] + ys[L:]

    inv_n = 1.0 / RWKV_HEAD
    mean = seg_sum(y) * inv_n
    yc = y - mean
    var = seg_sum(yc * yc) * inv_n
    yn = yc * lax.rsqrt(var + RWKV_GN_EPS) * gnw + gnb
    return yn + seg_sum(r * kmod * rkp) * v


def _run_in_lockstep(gens):
    results = [None] * len(gens)
    live = list(range(len(gens)))
    while live:
        still = []
        for i in live:
            try:
                next(gens[i])
                still.append(i)
            except StopIteration as done:
                results[i] = done.value
        live = still
    return results


def _rwkv_recur_kernel(G, L, r_ref, ld_ref, k_ref, v_ref, a_ref, kk_ref, ka_ref, rk_ref, gw_ref, gb_ref,
                       y_ref, s_ref):
    @pl.when(pl.program_id(0) == 0)
    def _():
        s_ref[...] = jnp.zeros_like(s_ref)

    B = r_ref.shape[0]
    blocks = [(b, slice(g * LANES, (g + 1) * LANES)) for b in range(B) for g in range(G)]
    ti = lax.broadcasted_iota(jnp.int32, (L, L), 0)
    tj = lax.broadcasted_iota(jnp.int32, (L, L), 1)
    tri = jnp.where(tj <= ti, 1.0, 0.0).astype(BF16)

    def chunk(c, carry):
        rows = pl.ds(pl.multiple_of(c * L, L), L)
        lds, cums = [], []
        for b in range(B):
            ld = ld_ref[b, rows, :]
            cum = jnp.zeros_like(ld)
            rest = ld
            for _ in range(3):
                part = rest.astype(BF16)
                cum = cum + jnp.dot(tri, part, preferred_element_type=F32)
                rest = rest - part.astype(F32)
            lds.append(ld)
            cums.append(cum)
        f32 = lambda ref, b, sl: ref[b, rows, sl].astype(F32)
        ys = _run_in_lockstep([
            _rwkv_pair_chunk(f32(r_ref, b, sl), lds[b][:, sl], cums[b][:, sl], f32(k_ref, b, sl),
                             f32(v_ref, b, sl), f32(a_ref, b, sl), kk_ref[:, sl], ka_ref[:, sl],
                             rk_ref[:, sl], gw_ref[:, sl], gb_ref[:, sl], s_ref.at[n])
            for n, (b, sl) in enumerate(blocks)])
        for (b, sl), y in zip(blocks, ys):
            y_ref[b, rows, sl] = y.astype(BF16)
        return carry

    lax.fori_loop(0, r_ref.shape[1] // L, chunk, 0)


def _rwkv_recur(r, ld, k, v, a, kkp, kap, rkp, gnw, gnb, B, T, chunks_per_step=4):
    N, D = r.shape
    L = RWKV_CHUNK
    G = D // LANES
    rows = L * chunks_per_step
    seq = pl.BlockSpec((B, rows, D), lambda c: (0, c, 0))
    par = pl.BlockSpec((1, D), lambda c: (0, 0))
    b3 = lambda t: t.reshape(B, T, D)
    y = pl.pallas_call(
        functools.partial(_rwkv_recur_kernel, G, L),
        grid=(T // rows,),
        in_specs=[seq] * 5 + [par] * 5,
        out_specs=seq,
        out_shape=jax.ShapeDtypeStruct((B, T, D), BF16),
        scratch_shapes=[pltpu.VMEM((B * G, LANES, LANES), F32)],
        compiler_params=_cparams(("arbitrary",)),
        name="rwkv_recur",
    )(b3(r), b3(ld), b3(k), b3(v), b3(a), kkp, kap, rkp, gnw, gnb)
    return y.reshape(N, D)


def _swiglu_chunk(u, w1, w3, w2):
    return _dot(_silu(_dot(u, w1)) * _dot(u, w3), w2)


def _dense_ffn_kernel(h_ref, y_ref, g_ref, wo_ref, nw_ref, w1_ref, w3_ref, w2_ref, o_ref, u_ref):
    h = h_ref[...] + _dot(y_ref[...].astype(F32) * g_ref[...].astype(F32), wo_ref[...])
    u_ref[...] = _rms(h, nw_ref[...]).astype(BF16)
    o_ref[...] = h
    for c in range(w1_ref.shape[1] // FFN_SUB):
        cs = slice(c * FFN_SUB, (c + 1) * FFN_SUB)
        o_ref[...] += _swiglu_chunk(u_ref[...], w1_ref[:, cs], w3_ref[:, cs], w2_ref[cs, :])


def _mix_out_dense_ffn(h, y, g, wo, nw, w1, w3, w2, tm=512):
    N, D = h.shape
    tm = min(tm, N)
    row = pl.BlockSpec((tm, D), lambda i: (i, 0))
    return pl.pallas_call(
        _dense_ffn_kernel,
        grid=(N // tm,),
        in_specs=[row, row, row] + [_full(w.shape) for w in (wo, nw, w1, w3, w2)],
        out_specs=row,
        out_shape=jax.ShapeDtypeStruct((N, D), F32),
        scratch_shapes=[pltpu.VMEM((tm, D), BF16)],
        compiler_params=_cparams(("parallel",)),
        name="dense_ffn",
    )(h, y, g, wo, nw, w1, w3, w2)


def _ret_proj_kernel(dk, h_ref, nw_ref, cos_ref, sin_ref, wq_ref, wk_ref, wv_ref, wg_ref,
                     q_out, k_out, v_out, g_out):
    u = _rms(h_ref[...], nw_ref[...]).astype(BF16)
    nh = wq_ref.shape[1] // dk
    cos = jnp.concatenate([cos_ref[...]] * nh, axis=1)
    sin = jnp.concatenate([sin_ref[...]] * nh, axis=1)
    lane = lax.broadcasted_iota(jnp.int32, cos.shape, 1)
    even = (lane & 1) == 0
    width = cos.shape[1]

    def rope(t):
        nxt = pltpu.roll(t, width - 1, 1)
        prv = pltpu.roll(t, 1, 1)
        return t * cos + jnp.where(even, nxt, prv) * sin

    q_out[...] = rope(_dot(u, wq_ref[...])).astype(BF16)
    k_out[...] = rope(_dot(u, wk_ref[...]) * (dk ** -0.5)).astype(BF16)
    v_out[...] = _dot(u, wv_ref[...]).astype(BF16)
    g_out[...] = _dot(u, wg_ref[...]).astype(BF16)


def _ret_proj(h, T, nw, cos, sin, wq, wk, wv, wg, tm=512):
    N, D = h.shape
    dk = cos.shape[1]
    tm = min(tm, T)
    tpb = T // tm
    row = lambda w: pl.BlockSpec((tm, w), lambda i: (i, 0))
    tab = pl.BlockSpec((tm, dk), lambda i: (i % tpb, 0))
    return pl.pallas_call(
        functools.partial(_ret_proj_kernel, dk),
        grid=(N // tm,),
        in_specs=[row(D), _full(nw.shape), tab, tab] + [_full(w.shape) for w in (wq, wk, wv, wg)],
        out_specs=[row(wq.shape[1]), row(wk.shape[1]), row(wv.shape[1]), row(wg.shape[1])],
        out_shape=[jax.ShapeDtypeStruct((N, w.shape[1]), BF16) for w in (wq, wk, wv, wg)],
        compiler_params=_cparams(("parallel",)),
        name="ret_proj",
    )(h, nw, cos, sin, wq, wk, wv, wg)


def _ret_head_chunk(q, k, v, g, intra, qd, kd, cd, r_ref):
    r0 = r_ref[...]
    s = _dot_nt(q, k) * intra
    inter = _dot(q, r0) * qd
    r_ref[...] = r0 * cd + _dot_tn(k.astype(F32) * kd, v)
    yield
    o = _dot(s, v) + inter
    yield
    o = o * lax.rsqrt(jnp.mean(o * o, axis=-1, keepdims=True) + RMS_EPS)
    return (_silu(g.astype(F32)) * o).astype(BF16)


def _ret_chunk_kernel(H, q_ref, k_ref, v_ref, g_ref, intra_ref, qd_ref, kd_ref, cd_ref, o_ref, r_ref):
    @pl.when(pl.program_id(1) == 0)
    def _():
        r_ref[...] = jnp.zeros_like(r_ref)

    dk = q_ref.shape[1] // H
    dv = v_ref.shape[1] // H
    ks = [slice(h * dk, (h + 1) * dk) for h in range(H)]
    vs = [slice(h * dv, (h + 1) * dv) for h in range(H)]
    outs = _run_in_lockstep([
        _ret_head_chunk(q_ref[:, ks[h]], k_ref[:, ks[h]], v_ref[:, vs[h]], g_ref[:, vs[h]],
                        intra_ref[h], qd_ref[h], kd_ref[h], cd_ref[h], r_ref.at[h])
        for h in range(H)])
    for h in range(H):
        o_ref[:, vs[h]] = outs[h]


def _ret_chunk(q, k, v, g, B, T, H):
    N = q.shape[0]
    dk = q.shape[1] // H
    dv = v.shape[1] // H
    C = min(RET_BLOCK, T)
    nc = T // C
    lg = jnp.log(1.0 - 2.0 ** (-5.0 - jnp.arange(H, dtype=F32)))
    n = jnp.arange(C, dtype=F32)
    diff = n[:, None] - n[None, :]
    intra = jnp.where(diff >= 0, jnp.exp(lg[:, None, None] * jnp.maximum(diff, 0.0)), 0.0)
    qd = jnp.broadcast_to(jnp.exp(lg[:, None] * (n[None, :] + 1.0))[:, :, None], (H, C, LANES))
    kd = jnp.broadcast_to(jnp.exp(lg[:, None] * (C - 1.0 - n[None, :]))[:, :, None], (H, C, LANES))
    cd = jnp.broadcast_to(jnp.exp(lg * C)[:, None, None], (H, 8, LANES))
    qd = jnp.concatenate([qd] * (dv // LANES), axis=2)
    kd = jnp.concatenate([kd] * (dk // LANES), axis=2)
    cd = jnp.concatenate([cd] * (dv // LANES), axis=2)[:, :1]
    seq = lambda w: pl.BlockSpec((C, w), lambda b, c: (b * nc + c, 0))
    return pl.pallas_call(
        functools.partial(_ret_chunk_kernel, H),
        grid=(B, nc),
        in_specs=[seq(H * dk), seq(H * dk), seq(H * dv), seq(H * dv)]
        + [_full(t.shape) for t in (intra, qd, kd, cd)],
        out_specs=seq(H * dv),
        out_shape=jax.ShapeDtypeStruct((N, H * dv), BF16),
        scratch_shapes=[pltpu.VMEM((H, dk, dv), F32)],
        compiler_params=_cparams(("parallel", "arbitrary")),
        name="ret_chunk",
    )(q, k, v, g, intra, qd, kd, cd)


def _ret_out_router_kernel(h_ref, o_ref, wo_ref, nw_ref, rt_ref, h_out, u_out, idx_out, gate_out):
    h = h_ref[...] + _dot(o_ref[...], wo_ref[...])
    h_out[...] = h
    u = _rms(h, nw_ref[...])
    u_out[...] = u
    logits = lax.dot_general(rt_ref[...], u, (((1,), (1,)), ((), ())),
                             precision=lax.Precision.HIGHEST, preferred_element_type=F32)
    e_iota = lax.broadcasted_iota(jnp.int32, logits.shape, 0)
    big = jnp.int32(N_EXPERTS)
    m1 = jnp.max(logits, axis=0, keepdims=True)
    i1 = jnp.min(jnp.where(logits == m1, e_iota, big), axis=0, keepdims=True)
    rest = jnp.where(e_iota == i1, -jnp.inf, logits)
    m2 = jnp.max(rest, axis=0, keepdims=True)
    i2 = jnp.min(jnp.where(rest == m2, e_iota, big), axis=0, keepdims=True)
    p2 = jnp.exp(m2 - m1)
    g1 = 1.0 / (1.0 + p2)
    idx_out[...] = jnp.concatenate([i1, i2], axis=0)
    gate_out[...] = jnp.concatenate([g1, p2 * g1], axis=0)


def _ret_out_router(h, o, wo, nw, router_t, tm=512):
    N, D = h.shape
    K = o.shape[1]
    tm = min(tm, N)
    row = lambda w: pl.BlockSpec((tm, w), lambda i: (i, 0))
    col = pl.BlockSpec((TOP_K, tm), lambda i: (0, i))
    return pl.pallas_call(
        _ret_out_router_kernel,
        grid=(N // tm,),
        in_specs=[row(D), row(K), _full(wo.shape), _full(nw.shape), _full(router_t.shape)],
        out_specs=[row(D), row(D), col, col],
        out_shape=[jax.ShapeDtypeStruct((N, D), F32), jax.ShapeDtypeStruct((N, D), F32),
                   jax.ShapeDtypeStruct((TOP_K, N), jnp.int32), jax.ShapeDtypeStruct((TOP_K, N), F32)],
        compiler_params=_cparams(("parallel",)),
        name="ret_out_router",
    )(h, o, wo, nw, router_t)


def _row_copy(src_hbm, dst_ref, sem, s, j):
    return pltpu.make_async_copy(src_hbm.at[pl.ds(s, 1), :], dst_ref.at[pl.ds(j, 1), :], sem)


def _moe_kernel(tmm, q, te_ref, src_ref, dst_ref, u_hbm, w1_ref, w3_ref, w2_ref, y_hbm,
                xbuf, xb_ref, obuf, gsem, ssem):
    i = pl.program_id(0)
    f = pl.program_id(1)
    n_i = pl.num_programs(0)
    n_f = pl.num_programs(1)
    cur = i % 2
    oth = 1 - cur

    def gather(slot, j, tok):
        return _row_copy(u_hbm, xbuf.at[slot], gsem.at[slot], tok, j)

    def scatter(slot, j, row):
        return pltpu.make_async_copy(obuf.at[slot].at[pl.ds(j, 1), :], y_hbm.at[pl.ds(row, 1), :],
                                     ssem.at[slot])

    def wait_rows(make):
        def body(j, c):
            make(j).wait()
            return c
        lax.fori_loop(0, tmm, body, 0, unroll=DMA_UNROLL)

    @pl.when((i == 0) & (f == 0))
    def _():
        obuf[1] = jnp.zeros((tmm, obuf.shape[2]), F32)

        def first(j, c):
            gather(0, j, te_ref[n_i + j]).start()
            return c
        lax.fori_loop(0, tmm, first, 0, unroll=DMA_UNROLL)

    @pl.when(f == 0)
    def _():
        wait_rows(lambda j: gather(cur, j, 0))
        xb_ref[...] = xbuf[cur].astype(BF16)

        @pl.when(i > 0)
        def _():
            wait_rows(lambda j: scatter(cur, j, 0))
        obuf[cur] = jnp.zeros((tmm, obuf.shape[2]), F32)

    def issue_rows(slot, first):
        for j in range(first, first + q):
            gather(slot, j, src_ref[j]).start()
            scatter(slot, j, dst_ref[j]).start(priority=j % 2)

    for slot in range(2):
        for ff in range(tmm // q):
            pl.when((oth == slot) & (f == ff))(functools.partial(issue_rows, slot, ff * q))

    @pl.when(i < te_ref[n_i + 2 * tmm])
    def _():
        for c in range(w1_ref.shape[2] // MOE_SUB):
            cs = slice(c * MOE_SUB, (c + 1) * MOE_SUB)
            obuf[cur] += _swiglu_chunk(xb_ref[...], w1_ref[0, :, cs], w3_ref[0, :, cs], w2_ref[0, cs, :])

    @pl.when((f == n_f - 1) & (i == n_i - 1))
    def _():
        def last(j, c):
            scatter(cur, j, te_ref[n_i + tmm + j]).start()
            return c
        lax.fori_loop(0, tmm, last, 0, unroll=DMA_UNROLL)
        wait_rows(lambda j: scatter(oth, j, 0))
        wait_rows(lambda j: scatter(cur, j, 0))
        wait_rows(lambda j: gather(oth, j, 0))


def _moe_experts(u, plan, src_next, dst_prev, w1, w3, w2, n_rows_out, tf=1792):
    N, D = u.shape
    Fd = w1.shape[2]
    tmm = MOE_TILE
    n_tiles = src_next.shape[0] // tmm
    nf = Fd // tf
    q = tmm // nf
    smem = lambda: pl.BlockSpec((tmm,), lambda i, f, te: (i,), memory_space=pltpu.SMEM)
    gs = pltpu.PrefetchScalarGridSpec(
        num_scalar_prefetch=1,
        grid=(n_tiles, nf),
        in_specs=[smem(), smem(), pl.BlockSpec(memory_space=pl.ANY),
                  pl.BlockSpec((1, D, tf), lambda i, f, te: (te[i], 0, f)),
                  pl.BlockSpec((1, D, tf), lambda i, f, te: (te[i], 0, f)),
                  pl.BlockSpec((1, tf, D), lambda i, f, te: (te[i], f, 0))],
        out_specs=pl.BlockSpec(memory_space=pl.ANY),
        scratch_shapes=[pltpu.VMEM((2, tmm, D), F32), pltpu.VMEM((tmm, D), BF16),
                        pltpu.VMEM((2, tmm, D), F32), pltpu.SemaphoreType.DMA((2,)),
                        pltpu.SemaphoreType.DMA((2,))],
    )
    return pl.pallas_call(
        functools.partial(_moe_kernel, tmm, q),
        grid_spec=gs,
        out_shape=jax.ShapeDtypeStruct((n_rows_out, D), F32),
        compiler_params=_cparams(("arbitrary", "arbitrary")),
        name="moe_experts",
    )(plan, src_next, dst_prev, u, w1, w3, w2)


def _combine_kernel(h_ref, y0_ref, y1_ref, gate_ref, nw_ref, o_ref):
    gate = gate_ref[...]
    moe = gate[:, 0:1] * y0_ref[...] + gate[:, 1:2] * y1_ref[...]
    o_ref[...] = _rms(h_ref[...] + moe, nw_ref[...])


def _combine(h, y, gate, nw, tc=512):
    N, D = h.shape
    tc = min(tc, N)
    nb = N // tc
    return pl.pallas_call(
        _combine_kernel,
        grid=(nb,),
        in_specs=[pl.BlockSpec((tc, D), lambda i: (i, 0)),
                  pl.BlockSpec((tc, D), lambda i: (i, 0)),
                  pl.BlockSpec((tc, D), lambda i: (nb + i, 0)),
                  pl.BlockSpec((tc, TOP_K), lambda i: (i, 0)),
                  _full(nw.shape)],
        out_specs=pl.BlockSpec((tc, D), lambda i: (i, 0)),
        out_shape=jax.ShapeDtypeStruct((N, D), F32),
        compiler_params=_cparams(("parallel",)),
        name="moe_combine",
    )(h, y, y, gate, nw)


def _route_plan(idx, N):
    tmm = MOE_TILE
    n_asg = TOP_K * N
    n_tiles = n_asg // tmm + N_EXPERTS
    R = n_tiles * tmm
    flat_e = idx.reshape(-1)
    asg = jnp.arange(n_asg, dtype=jnp.int32)
    order = jnp.sort(flat_e * n_asg + asg) % n_asg
    counts = jnp.sum((flat_e[None, :] == jnp.arange(N_EXPERTS, dtype=jnp.int32)[:, None]).astype(jnp.int32),
                     axis=1)
    ptiles = (counts + tmm - 1) // tmm
    tile_end = jnp.cumsum(ptiles)
    tile_start = tile_end - ptiles
    cum_incl = jnp.cumsum(counts)
    cum_excl = cum_incl - counts
    t = jnp.arange(n_tiles, dtype=jnp.int32)
    tile_e = jnp.sum((t[:, None] >= tile_end[None, :]).astype(jnp.int32), axis=1)
    tile_expert = jnp.minimum(tile_e, N_EXPERTS - 1)
    span = n_asg + tmm
    order_pad = jnp.concatenate([order, jnp.zeros((span,), jnp.int32)])
    rows = jnp.zeros((R + span,), jnp.int32)
    for e in range(N_EXPERTS):
        run = lax.dynamic_slice(order_pad, (cum_excl[e],), (span,))
        rows = lax.dynamic_update_slice(rows, run, (tile_start[e] * tmm,))
    k = t - tile_start[tile_expert]
    n_real = jnp.where(tile_e < N_EXPERTS, jnp.clip(counts[tile_expert] - k * tmm, 0, tmm), 0)
    j = jnp.arange(tmm, dtype=jnp.int32)
    real_before = jnp.concatenate([cum_incl, jnp.full((1,), n_asg, jnp.int32)])[tile_e]
    spare = n_asg + (t * tmm - real_before)[:, None] + j[None, :]
    dst = jnp.where(j[None, :] < n_real[:, None], rows[:R].reshape(n_tiles, tmm), spare).reshape(R)
    src = jnp.where(dst < n_asg, dst % N, 0)
    zeros = jnp.zeros((tmm,), jnp.int32)
    before_first = R + jnp.arange(tmm, dtype=jnp.int32)
    src_next = jnp.concatenate([src[tmm:], zeros])
    dst_prev = jnp.concatenate([before_first, dst[:R - tmm]])
    plan = jnp.concatenate([tile_expert, src[:tmm], dst[R - tmm:], tile_end[-1:]]).astype(jnp.int32)
    return plan, src_next, dst_prev, R + tmm


def _rope_tables(T, dk):
    angle = 1.0 / (ROPE_BASE ** jnp.linspace(0.0, 1.0, dk // 2, dtype=F32))
    angle = jnp.repeat(angle, 2)
    ang = jnp.arange(T, dtype=F32)[:, None] * angle[None, :]
    sign = jnp.where(jnp.arange(dk) % 2 == 0, -1.0, 1.0).astype(F32)
    return jnp.cos(ang), jnp.sin(ang) * sign[None, :]


def kernel(x, norm_mix, norm_ffn, norm_final, a_mu, a_wr, a_wk, a_wv, a_wo, a_w0, a_w1, a_w2, a_a0, a_a1, a_a2, a_g1, a_g2, a_kk, a_ka, a_rk, a_gn_w, a_gn_b, b_wq, b_wk, b_wv, b_wg, b_wo, f_w1, f_w3, f_w2, m_router, m_w1, m_w3, m_w2):
    B, T, D = x.shape
    N = B * T
    bf = lambda w: w.astype(BF16)
    vec = lambda w: w.reshape(1, -1).astype(F32)
    h = x.reshape(N, D)

    r, ld, k, v, a, g = _rwkv_proj(
        h, T, vec(norm_mix[0]), a_mu[0], bf(a_wr[0]), bf(a_wk[0]), bf(a_wv[0]), bf(a_w1[0]), bf(a_w2[0]),
        bf(a_a1[0]), bf(a_a2[0]), bf(a_g1[0]), bf(a_g2[0]), vec(a_w0[0]), vec(a_a0[0]))
    y = _rwkv_recur(r, ld, k, v, a, vec(a_kk[0]), vec(a_ka[0]), vec(a_rk[0]), vec(a_gn_w[0]),
                    vec(a_gn_b[0]), B, T)
    h = _mix_out_dense_ffn(h, y, g, bf(a_wo[0]), vec(norm_ffn[0]), bf(f_w1[0]), bf(f_w3[0]), bf(f_w2[0]))

    dk = b_wq.shape[2] // RET_HEADS
    cos, sin = _rope_tables(T, dk)
    q, kr, vr, gr = _ret_proj(h, T, vec(norm_mix[1]), cos, sin, bf(b_wq[0]), bf(b_wk[0]), bf(b_wv[0]),
                              bf(b_wg[0]))
    o = _ret_chunk(q, kr, vr, gr, B, T, RET_HEADS)
    h, u, idx, gate = _ret_out_router(h, o, bf(b_wo[0]), vec(norm_ffn[1]), m_router[0].T.astype(F32))
    plan, src_next, dst_prev, n_rows = _route_plan(idx, N)
    y = _moe_experts(u, plan, src_next, dst_prev, bf(m_w1[0]), bf(m_w3[0]), bf(m_w2[0]), n_rows)
    out = _combine(h, y, gate.T, vec(norm_final))
    return out.reshape(B, T, D)
```

```python
import functools
import math

import jax
import jax.numpy as jnp
from jax import lax
from jax.experimental import pallas as pl
from jax.experimental.pallas import tpu as pltpu

F32 = jnp.float32
BF16 = jnp.bfloat16

RMS_EPS = 1e-6
RWKV_HEAD = 64
RWKV_GN_EPS = 64e-5
RET_HEADS = 4
ROPE_BASE = 10000.0
N_EXPERTS = 8
TOP_K = 2

LANES = 128
VMEM_LIMIT = 56 * 1024 * 1024

RWKV_CHUNK = 64
RWKV_SUB = 16
RET_BLOCK = 256
MOE_TILE = 512
MOE_SUB = 256
FFN_SUB = 256
DMA_UNROLL = 8


def _cparams(sem):
    return pltpu.CompilerParams(dimension_semantics=sem, vmem_limit_bytes=VMEM_LIMIT)


def _dot(a, b):
    return jnp.dot(a.astype(BF16), b.astype(BF16), preferred_element_type=F32)


def _dot_nt(a, b):
    return lax.dot_general(a.astype(BF16), b.astype(BF16), (((1,), (1,)), ((), ())),
                           preferred_element_type=F32)


def _dot_tn(a, b):
    return lax.dot_general(a.astype(BF16), b.astype(BF16), (((0,), (0,)), ((), ())),
                           preferred_element_type=F32)


def _rms(x, g):
    return x * lax.rsqrt(jnp.mean(x * x, axis=-1, keepdims=True) + RMS_EPS) * g


def _sigmoid(x):
    return 1.0 / (1.0 + jnp.exp(-x))


def _silu(x):
    return x * _sigmoid(x)


def _full(shape):
    n = len(shape)
    return pl.BlockSpec(shape, lambda *_: (0,) * n, pipeline_mode=pl.Buffered(1))


def _rwkv_proj_kernel(T, tm, h_ref, hp_ref, nw_ref, mu_ref, wr_ref, wk_ref, wv_ref, w1_ref, w2_ref,
                      a1_ref, a2_ref, g1_ref, g2_ref, w0_ref, a0_ref,
                      r_out, ld_out, k_out, v_out, a_out, g_out):
    i = pl.program_id(0)
    nw = nw_ref[...]
    u = _rms(h_ref[...], nw)
    up = _rms(hp_ref[...], nw)
    first = (i * tm) % T == 0
    prev_last = jnp.where(first, 0.0, up[7:8, :])
    row = lax.broadcasted_iota(jnp.int32, u.shape, 0)
    us = jnp.where(row == 0, prev_last, pltpu.roll(u, 1, 0))
    dx = us - u

    def mix(j):
        return (u + dx * mu_ref[j:j + 1, :]).astype(BF16)

    r_out[...] = _dot(mix(0), wr_ref[...]).astype(BF16)
    z =w0_ref[...] + _dot(jnp.tanh(_dot(mix(1), w1_ref[...])), w2_ref[...])
    nz = -z
    softplus = jnp.maximum(nz, 0.0) + jnp.log(1.0 + jnp.exp(-jnp.abs(nz)))
    ld_out[...] = -jnp.exp(-softplus - 0.5)
    k_out[...] = _dot(mix(2), wk_ref[...]).astype(BF16)
    v_out[...] = _dot(mix(3), wv_ref[...]).astype(BF16)
    a_out[...] = _sigmoid(a0_ref[...] + _dot(_dot(mix(4), a1_ref[...]), a2_ref[...])).astype(BF16)
    g_out[...] = _dot(_sigmoid(_dot(mix(5), g1_ref[...])), g2_ref[...]).astype(BF16)


def _rwkv_proj(h, T, nw, mu, wr, wk, wv, w1, w2, a1, a2, g1, g2, w0, a0, tm=512):
    N, D = h.shape
    tm = min(tm, T)
    row = pl.BlockSpec((tm, D), lambda i: (i, 0))
    prev = pl.BlockSpec((8, D), lambda i: (jnp.maximum(i * (tm // 8) - 1, 0), 0))
    ws = [nw, mu, wr, wk, wv, w1, w2, a1, a2, g1, g2, w0, a0]
    out = lambda dt: jax.ShapeDtypeStruct((N, D), dt)
    return pl.pallas_call(
        functools.partial(_rwkv_proj_kernel, T, tm),
        grid=(N // tm,),
        in_specs=[row, prev] + [_full(w.shape) for w in ws],
        out_specs=[row] * 6,
        out_shape=[out(BF16), out(F32), out(BF16), out(BF16), out(BF16), out(BF16)],
        compiler_params=_cparams(("parallel",)),
        name="rwkv_proj",
    )(h, h, *ws)


def _rwkv_pair_chunk(r, ld, cum, k, v, a, kkp, kap, rkp, gnw, gnb, s_ref):
    L = r.shape[0]
    L2 = 2 * L
    lane = lax.broadcasted_iota(jnp.int32, (L, LANES), 1)
    m0 = lane < RWKV_HEAD

    def seg_sum(x):
        s0 = jnp.sum(jnp.where(m0, x, 0.0), axis=-1, keepdims=True)
        s1 = jnp.sum(jnp.where(m0, 0.0, x), axis=-1, keepdims=True)
        return jnp.where(m0, s0, s1)

    def stack(x):
        return jnp.concatenate([jnp.where(m0, x, 0.0), jnp.where(m0, 0.0, x)], axis=0)

    kk = k * kkp
    kk = kk / jnp.maximum(jnp.sqrt(seg_sum(kk * kk)), 1e-12)
    kmod = k * (1.0 + (a - 1.0) * kap)
    aa = -kk
    bb = kk * a

    tot = cum[L - 1:L, :]
    e_cum = jnp.exp(cum)
    e_inv = jnp.exp(-cum)
    e_fin = jnp.exp(tot - cum)
    a_t = stack(aa * jnp.exp(cum - ld))
    r_t = stack(r * e_cum)
    b_t = stack(bb * e_inv)
    k_t = stack(kmod * e_inv)
    b_h = stack(bb * e_fin)
    k_h = stack(kmod * e_fin)
    v_s = stack(v)

    mm = _dot_nt(jnp.concatenate([a_t, r_t], axis=0), jnp.concatenate([b_t, k_t], axis=0))
    yield
    si =lax.broadcasted_iota(jnp.int32, (L2, L2), 0)
    sj = lax.broadcasted_iota(jnp.int32, (L2, L2), 1)
    low = sj < si
    lowi = sj <= si
    m_ab = jnp.where(low, mm[:L2, :L2], 0.0)
    m_ak = jnp.where(low, mm[:L2, L2:], 0.0)
    m_rb = jnp.where(lowi, mm[L2:, :L2], 0.0)
    m_rk = jnp.where(lowi, mm[L2:, L2:], 0.0)

    eye = jnp.where(si == sj, 1.0, 0.0).astype(F32)
    sub_shift = RWKV_SUB.bit_length() - 1
    same = (si >> sub_shift) == (sj >> sub_shift)
    dg = jnp.where(same, m_ab, 0.0)
    og = jnp.where(same, 0.0, m_ab)
    d2 = _dot(dg, dg)
    s0 = s_ref[...]
    ar_s = _dot_nt(jnp.concatenate([a_t, r_t], axis=0), s0)
    c = ar_s[:L2] + _dot(m_ak, v_s)
    yield
    d4 = _dot(d2, d2)
    x1 = eye + dg + d2 + _dot(dg, d2)
    yield
    d8 = _dot(d4, d4)
    yield
    x2 = eye + d4 + d8 + _dot(d4, d8)
    yield
    t_d = _dot(x1, x2)
    yield
    e1 = _dot(t_d, og)
    yield
    e2 = _dot(e1, e1)
    yield
    t_f = eye + e1 + e2 + _dot(e1, e2)
    yield
    t_m = _dot(t_f, t_d)
    yield
    u = _dot(t_m, c)
    yield
    uv = jnp.concatenate([u, v_s], axis=0)
    ys = ar_s[L2:] + _dot(jnp.concatenate([m_rb, m_rk], axis=1), uv)
    s_ref[...] = s0 * jnp.exp(tot) + _dot_tn(uv, jnp.concatenate([b_h, k_h], axis=0))
    yield
    y = ys[:L] + ys[L:]

    inv_n = 1.0 / RWKV_HEAD
    mean = seg_sum(y) * inv_n
    yc = y - mean
    var = seg_sum(yc * yc) * inv_n
    yn = yc * lax.rsqrt(var + RWKV_GN_EPS) * gnw + gnb
    return yn + seg_sum(r * kmod * rkp) * v


def _run_in_lockstep(gens):
    results = [None] * len(gens)
    live = list(range(len(gens)))
    while live:
        still = []
        for i in live:
            try:
                next(gens[i])
                still.append(i)
            except StopIteration as done:
                results[i] = done.value
        live = still
    return results


def _rwkv_recur_kernel(G, L, r_ref, ld_ref, k_ref, v_ref, a_ref, kk_ref, ka_ref, rk_ref, gw_ref, gb_ref,
                       y_ref, s_ref):
    @pl.when(pl.program_id(0) == 0)
    def _():
        s_ref[...] = jnp.zeros_like(s_ref)

    B = r_ref.shape[0]
    blocks = [(b, slice(g * LANES, (g + 1) * LANES)) for b in range(B) for g in range(G)]
    ti = lax.broadcasted_iota(jnp.int32, (L, L), 0)
    tj = lax.broadcasted_iota(jnp.int32, (L, L), 1)
    tri = jnp.where(tj <= ti, 1.0, 0.0).astype(BF16)

    def chunk(c, carry):
        rows = pl.ds(pl.multiple_of(c * L, L), L)
        lds, cums = [], []
        for b in range(B):
            ld = ld_ref[b, rows, :]
            cum = jnp.zeros_like(ld)
            rest = ld
            for _ in range(3):
                part = rest.astype(BF16)
                cum = cum + jnp.dot(tri, part, preferred_element_type=F32)
                rest = rest - part.astype(F32)
            lds.append(ld)
            cums.append(cum)
        f32 = lambda ref, b, sl: ref[b, rows, sl].astype(F32)
        ys = _run_in_lockstep([
            _rwkv_pair_chunk(f32(r_ref, b, sl), lds[b][:, sl], cums[b][:, sl], f32(k_ref, b, sl),
                             f32(v_ref, b, sl), f32(a_ref, b, sl), kk_ref[:, sl], ka_ref[:, sl],
                             rk_ref[:, sl], gw_ref[:, sl], gb_ref[:, sl], s_ref.at[n])
            for n, (b, sl) in enumerate(blocks)])
        for (b, sl), y in zip(blocks, ys):
            y_ref[b, rows, sl] = y.astype(BF16)
        return carry

    lax.fori_loop(0, r_ref.shape[1] // L, chunk, 0)


def _rwkv_recur(r, ld, k, v, a, kkp, kap, rkp, gnw, gnb, B, T, chunks_per_step=4):
    N, D = r.shape
    L = RWKV_CHUNK
    G = D // LANES
    rows = L * chunks_per_step
    seq = pl.BlockSpec((B, rows, D), lambda c: (0, c, 0))
    par = pl.BlockSpec((1, D), lambda c: (0, 0))
    b3 = lambda t: t.reshape(B, T, D)
    y = pl.pallas_call(
        functools.partial(_rwkv_recur_kernel, G, L),
        grid=(T // rows,),
        in_specs=[seq] * 5 + [par] * 5,
        out_specs=seq,
        out_shape=jax.ShapeDtypeStruct((B, T, D), BF16),
        scratch_shapes=[pltpu.VMEM((B * G, LANES, LANES), F32)],
        compiler_params=_cparams(("arbitrary",)),
        name="rwkv_recur",
    )(b3(r), b3(ld), b3(k), b3(v), b3(a), kkp, kap, rkp, gnw, gnb)
    return y.reshape(N, D)


def _swiglu_chunk(u, w1, w3, w2):
    return _dot(_silu(_dot(u, w1)) * _dot(u, w3), w2)


def _dense_ffn_kernel(h_ref, y_ref, g_ref, wo_ref, nw_ref, w1_ref, w3_ref, w2_ref, o_ref, u_ref):
    h = h_ref[...] + _dot(y_ref[...].astype(F32) * g_ref[...].astype(F32), wo_ref[...])
    u_ref[...] = _rms(h, nw_ref[...]).astype(BF16)
    o_ref[...] = h
    for c in range(w1_ref.shape[1] // FFN_SUB):
        cs = slice(c * FFN_SUB, (c + 1) * FFN_SUB)
        o_ref[...] += _swiglu_chunk(u_ref[...], w1_ref[:, cs], w3_ref[:, cs], w2_ref[cs, :])


def _mix_out_dense_ffn(h, y, g, wo, nw, w1, w3, w2, tm=512):
    N, D = h.shape
    tm = min(tm, N)
    row = pl.BlockSpec((tm, D), lambda i: (i, 0))
    return pl.pallas_call(
        _dense_ffn_kernel,
        grid=(N // tm,),
        in_specs=[row, row, row] + [_full(w.shape) for w in (wo, nw, w1, w3, w2)],
        out_specs=row,
        out_shape=jax.ShapeDtypeStruct((N, D), F32),
        scratch_shapes=[pltpu.VMEM((tm, D), BF16)],
        compiler_params=_cparams(("parallel",)),
        name="dense_ffn",
    )(h, y, g, wo, nw, w1, w3, w2)


def _ret_proj_kernel(dk, h_ref, nw_ref, cs_ref, ss_ref, co_ref, so_ref, wq_ref, wk_ref, wv_ref, wg_ref,
                     q_out, k_out, v_out, g_out):
    u = _rms(h_ref[...], nw_ref[...]).astype(BF16)
    nh = wq_ref.shape[1] // dk
    cs, ss, co, so = cs_ref[0], ss_ref[0], co_ref[...], so_ref[...]
    cos = jnp.concatenate([cs * co - ss * so] * nh, axis=1)
    sin = jnp.concatenate([ss * co + cs * so] * nh, axis=1)
    lane = lax.broadcasted_iota(jnp.int32, cos.shape, 1)
    even = (lane & 1) == 0
    width = cos.shape[1]

    def rope(t):
        nxt = pltpu.roll(t, width - 1, 1)
        prv = pltpu.roll(t, 1, 1)
        return t * cos + jnp.where(even, nxt, prv) * sin

    q_out[...] = rope(_dot(u, wq_ref[...])).astype(BF16)
    k_out[...] = rope(_dot(u, wk_ref[...]) * (dk ** -0.5)).astype(BF16)
    v_out[...] = _dot(u, wv_ref[...]).astype(BF16)
    g_out[...] = _dot(u, wg_ref[...]).astype(BF16)


def _ret_proj(h, T, nw, wq, wk, wv, wg, tm=512):
    N, D = h.shape
    dk = wq.shape[1] // RET_HEADS
    tm = min(tm, T)
    tpb = T // tm
    cs, ss, co, so = _rope_tables(T, dk, tm)
    row = lambda w: pl.BlockSpec((tm, w), lambda i: (i, 0))
    start = pl.BlockSpec((1, 1, dk), lambda i: (i % tpb, 0, 0))
    return pl.pallas_call(
        functools.partial(_ret_proj_kernel, dk),
        grid=(N // tm,),
        in_specs=[row(D), _full(nw.shape), start, start, _full(co.shape), _full(so.shape)]
        + [_full(w.shape) for w in (wq, wk, wv, wg)],
        out_specs=[row(wq.shape[1]), row(wk.shape[1]), row(wv.shape[1]), row(wg.shape[1])],
        out_shape=[jax.ShapeDtypeStruct((N, w.shape[1]), BF16) for w in (wq, wk, wv, wg)],
        compiler_params=_cparams(("parallel",)),
        name="ret_proj",
    )(h, nw, cs, ss, co, so, wq, wk, wv, wg)


def _ret_head_chunk(q, k, v, g, intra, qd, kd, cd, r_ref):
    r0 = r_ref[...]
    s = _dot_nt(q, k) * intra
    inter = _dot(q, r0) * qd
    r_ref[...] = r0 * cd + _dot_tn(k.astype(F32) * kd, v)
    yield
    o = _dot(s, v) + inter
    yield
    o = o * lax.rsqrt(jnp.mean(o * o, axis=-1, keepdims=True) + RMS_EPS)
    return (_silu(g.astype(F32)) * o).astype(BF16)


def _ret_chunk_kernel(H, q_ref, k_ref, v_ref, g_ref, intra_ref, qd_ref, kd_ref, cd_ref, o_ref, r_ref):
    @pl.when(pl.program_id(1) == 0)
    def _():
        r_ref[...] = jnp.zeros_like(r_ref)

    dk = q_ref.shape[1] // H
    dv = v_ref.shape[1] // H
    ks = [slice(h * dk, (h + 1) * dk) for h in range(H)]
    vs = [slice(h * dv, (h + 1) * dv) for h in range(H)]
    outs = _run_in_lockstep([
        _ret_head_chunk(q_ref[:, ks[h]], k_ref[:, ks[h]], v_ref[:, vs[h]], g_ref[:, vs[h]],
                        intra_ref[h], qd_ref[h], kd_ref[h], cd_ref[h], r_ref.at[h])
        for h in range(H)])
    for h in range(H):
        o_ref[:, vs[h]] = outs[h]


def _ret_chunk(q, k, v, g, B, T, H):
    N = q.shape[0]
    dk = q.shape[1] // H
    dv = v.shape[1] // H
    C = min(RET_BLOCK, T)
    nc = T // C
    lg = jnp.log(1.0 - 2.0 ** (-5.0 - jnp.arange(H, dtype=F32)))
    n = jnp.arange(C, dtype=F32)
    diff = n[:, None] - n[None, :]
    intra = jnp.where(diff >= 0, jnp.exp(lg[:, None, None] * jnp.maximum(diff, 0.0)), 0.0)
    qd = jnp.broadcast_to(jnp.exp(lg[:, None] * (n[None, :] + 1.0))[:, :, None], (H, C, LANES))
    kd = jnp.broadcast_to(jnp.exp(lg[:, None] * (C - 1.0 - n[None, :]))[:, :, None], (H, C, LANES))
    cd = jnp.broadcast_to(jnp.exp(lg * C)[:, None, None], (H, 8, LANES))
    qd = jnp.concatenate([qd] * (dv // LANES), axis=2)
    kd = jnp.concatenate([kd] * (dk // LANES), axis=2)
    cd = jnp.concatenate([cd] * (dv // LANES), axis=2)[:, :1]
    seq = lambda w: pl.BlockSpec((C, w), lambda b, c: (b * nc + c, 0))
    return pl.pallas_call(
        functools.partial(_ret_chunk_kernel, H),
        grid=(B, nc),
        in_specs=[seq(H * dk), seq(H * dk), seq(H * dv), seq(H * dv)]
        + [_full(t.shape) for t in (intra, qd, kd, cd)],
        out_specs=seq(H * dv),
        out_shape=jax.ShapeDtypeStruct((N, H * dv), BF16),
        scratch_shapes=[pltpu.VMEM((H, dk, dv), F32)],
        compiler_params=_cparams(("parallel", "arbitrary")),
        name="ret_chunk",
    )(q, k, v, g, intra, qd, kd, cd)


def _ret_out_router_kernel(h_ref, o_ref, wo_ref, nw_ref, rt_ref, h_out, u_out, idx_out, gate_out):
    h = h_ref[...] + _dot(o_ref[...], wo_ref[...])
    h_out[...] = h
    u = _rms(h, nw_ref[...])
    u_out[...] = u
    logits = lax.dot_general(rt_ref[...], u, (((1,), (1,)), ((), ())),
                             precision=lax.Precision.HIGHEST, preferred_element_type=F32)
    e_iota = lax.broadcasted_iota(jnp.int32, logits.shape, 0)
    big = jnp.int32(N_EXPERTS)
    m1 = jnp.max(logits, axis=0, keepdims=True)
    i1 = jnp.min(jnp.where(logits == m1, e_iota, big), axis=0, keepdims=True)
    rest = jnp.where(e_iota == i1, -jnp.inf, logits)
    m2 = jnp.max(rest, axis=0, keepdims=True)
    i2 = jnp.min(jnp.where(rest == m2, e_iota, big), axis=0, keepdims=True)
    p2 = jnp.exp(m2 - m1)
    g1 = 1.0 / (1.0 + p2)
    idx_out[...] = jnp.concatenate([i1, i2], axis=0)
    gate_out[...] = jnp.concatenate([g1, p2 * g1], axis=0)


def _ret_out_router(h, o, wo, nw, router_t, tm=512):
    N, D = h.shape
    K = o.shape[1]
    tm = min(tm, N)
    row = lambda w: pl.BlockSpec((tm, w), lambda i: (i, 0))
    col = pl.BlockSpec((TOP_K, tm), lambda i: (0, i))
    return pl.pallas_call(
        _ret_out_router_kernel,
        grid=(N // tm,),
        in_specs=[row(D), row(K), _full(wo.shape), _full(nw.shape), _full(router_t.shape)],
        out_specs=[row(D), row(D), col, col],
        out_shape=[jax.ShapeDtypeStruct((N, D), F32), jax.ShapeDtypeStruct((N, D), F32),
                   jax.ShapeDtypeStruct((TOP_K, N), jnp.int32), jax.ShapeDtypeStruct((TOP_K, N), F32)],
        compiler_params=_cparams(("parallel",)),
        name="ret_out_router",
    )(h, o, wo, nw, router_t)


def _row_copy(src_hbm, dst_ref, sem, s, j):
    return pltpu.make_async_copy(src_hbm.at[pl.ds(s, 1), :], dst_ref.at[pl.ds(j, 1), :], sem)


def _moe_kernel(tmm, q, te_ref, src_ref, dst_ref, u_hbm, w1_ref, w3_ref, w2_ref, y_hbm,
                xbuf, xb_ref, obuf, gsem, ssem):
    i = pl.program_id(0)
    f = pl.program_id(1)
    n_i = pl.num_programs(0)
    n_f = pl.num_programs(1)
    cur = i % 2
    oth = 1 - cur

    def gather(slot, j, tok):
        return _row_copy(u_hbm, xbuf.at[slot], gsem.at[slot], tok, j)

    def scatter(slot, j, row):
        return pltpu.make_async_copy(obuf.at[slot].at[pl.ds(j, 1), :], y_hbm.at[pl.ds(row, 1), :],
                                     ssem.at[slot])

    def wait_rows(make):
        def body(j, c):
            make(j).wait()
            return c
        lax.fori_loop(0, tmm, body, 0, unroll=DMA_UNROLL)

    @pl.when((i == 0) & (f == 0))
    def _():
        obuf[1] = jnp.zeros((tmm, obuf.shape[2]), F32)

        def first(j, c):
            gather(0, j, te_ref[n_i + j]).start()
            return c
        lax.fori_loop(0, tmm, first, 0, unroll=DMA_UNROLL)

    @pl.when(f == 0)
    def _():
        wait_rows(lambda j: gather(cur, j, 0))
        xb_ref[...] = xbuf[cur].astype(BF16)

        @pl.when(i > 0)
        def _():
            wait_rows(lambda j: scatter(cur, j, 0))
        obuf[cur] = jnp.zeros((tmm, obuf.shape[2]), F32)

    def issue_rows(slot, first):
        for j in range(first, first + q):
            gather(slot, j, src_ref[j]).start()
            scatter(slot, j, dst_ref[j]).start(priority=j % 2)

    for slot in range(2):
        for ff in range(tmm // q):
            pl.when((oth == slot) & (f == ff))(functools.partial(issue_rows, slot, ff * q))

    @pl.when(i < te_ref[n_i + 2 * tmm])
    def _():
        for c in range(w1_ref.shape[2] // MOE_SUB):
            cs = slice(c * MOE_SUB, (c + 1) * MOE_SUB)
            obuf[cur] += _swiglu_chunk(xb_ref[...], w1_ref[0, :, cs], w3_ref[0, :, cs], w2_ref[0, cs, :])

    @pl.when((f == n_f - 1) & (i == n_i - 1))
    def _():
        def last(j, c):
            scatter(cur, j, te_ref[n_i + tmm + j]).start()
            return c
        lax.fori_loop(0, tmm, last, 0, unroll=DMA_UNROLL)
        wait_rows(lambda j: scatter(oth, j, 0))
        wait_rows(lambda j: scatter(cur, j, 0))
        wait_rows(lambda j: gather(oth, j, 0))


def _moe_experts(u, plan, src_next, dst_prev, w1, w3, w2, n_rows_out, tf=1792):
    N, D = u.shape
    Fd = w1.shape[2]
    tmm = MOE_TILE
    n_tiles = src_next.shape[0] // tmm
    nf = Fd // tf
    q = tmm // nf
    smem = lambda: pl.BlockSpec((tmm,), lambda i, f, te: (i,), memory_space=pltpu.SMEM)
    gs = pltpu.PrefetchScalarGridSpec(
        num_scalar_prefetch=1,
        grid=(n_tiles, nf),
        in_specs=[smem(), smem(), pl.BlockSpec(memory_space=pl.ANY),
                  pl.BlockSpec((1, D, tf), lambda i, f, te: (te[i], 0, f)),
                  pl.BlockSpec((1, D, tf), lambda i, f, te: (te[i], 0, f)),
                  pl.BlockSpec((1, tf, D), lambda i, f, te: (te[i], f, 0))],
        out_specs=pl.BlockSpec(memory_space=pl.ANY),
        scratch_shapes=[pltpu.VMEM((2, tmm, D), F32), pltpu.VMEM((tmm, D), BF16),
                        pltpu.VMEM((2, tmm, D), F32), pltpu.SemaphoreType.DMA((2,)),
                        pltpu.SemaphoreType.DMA((2,))],
    )
    return pl.pallas_call(
        functools.partial(_moe_kernel, tmm, q),
        grid_spec=gs,
        out_shape=jax.ShapeDtypeStruct((n_rows_out, D), F32),
        compiler_params=_cparams(("arbitrary", "arbitrary")),
        name="moe_experts",
    )(plan, src_next, dst_prev, u, w1, w3, w2)


def _combine_kernel(h_ref, y0_ref, y1_ref, gate_ref, nw_ref, o_ref):
    gate = gate_ref[...]
    moe = gate[:, 0:1] * y0_ref[...] + gate[:, 1:2] * y1_ref[...]
    o_ref[...] = _rms(h_ref[...] + moe, nw_ref[...])


def _combine(h, y, gate, nw, tc=512):
    N, D = h.shape
    tc = min(tc, N)
    nb = N // tc
    return pl.pallas_call(
        _combine_kernel,
        grid=(nb,),
        in_specs=[pl.BlockSpec((tc, D), lambda i: (i, 0)),
                  pl.BlockSpec((tc, D), lambda i: (i, 0)),
                  pl.BlockSpec((tc, D), lambda i: (nb + i, 0)),
                  pl.BlockSpec((tc, TOP_K), lambda i: (i, 0)),
                  _full(nw.shape)],
        out_specs=pl.BlockSpec((tc, D), lambda i: (i, 0)),
        out_shape=jax.ShapeDtypeStruct((N, D), F32),
        compiler_params=_cparams(("parallel",)),
        name="moe_combine",
    )(h, y, y, gate, nw)


def _route_plan(idx, N):
    tmm = MOE_TILE
    n_asg = TOP_K * N
    n_tiles = n_asg // tmm + N_EXPERTS
    R = n_tiles * tmm
    flat_e = idx.reshape(-1)
    asg = jnp.arange(n_asg, dtype=jnp.int32)
    order = jnp.sort(flat_e * n_asg + asg) % n_asg
    counts = jnp.sum((flat_e[None, :] == jnp.arange(N_EXPERTS, dtype=jnp.int32)[:, None]).astype(jnp.int32),
                     axis=1)
    ptiles = (counts + tmm - 1) // tmm
    tile_end = jnp.cumsum(ptiles)
    tile_start = tile_end - ptiles
    cum_incl = jnp.cumsum(counts)
    cum_excl = cum_incl - counts
    t = jnp.arange(n_tiles, dtype=jnp.int32)
    tile_e = jnp.sum((t[:, None] >= tile_end[None, :]).astype(jnp.int32), axis=1)
    tile_expert = jnp.minimum(tile_e, N_EXPERTS - 1)
    span = n_asg + tmm
    order_pad = jnp.concatenate([order, jnp.zeros((span,), jnp.int32)])
    rows = jnp.zeros((R + span,), jnp.int32)
    for e in range(N_EXPERTS):
        run = lax.dynamic_slice(order_pad, (cum_excl[e],), (span,))
        rows = lax.dynamic_update_slice(rows, run, (tile_start[e] * tmm,))
    k = t - tile_start[tile_expert]
    n_real = jnp.where(tile_e < N_EXPERTS, jnp.clip(counts[tile_expert] - k * tmm, 0, tmm), 0)
    j = jnp.arange(tmm, dtype=jnp.int32)
    real_before = jnp.concatenate([cum_incl, jnp.full((1,), n_asg, jnp.int32)])[tile_e]
    spare = n_asg + (t * tmm - real_before)[:, None] + j[None, :]
    dst = jnp.where(j[None, :] < n_real[:, None], rows[:R].reshape(n_tiles, tmm), spare).reshape(R)
    src = jnp.where(dst < n_asg, dst % N, 0)
    zeros = jnp.zeros((tmm,), jnp.int32)
    before_first = R + jnp.arange(tmm, dtype=jnp.int32)
    src_next = jnp.concatenate([src[tmm:], zeros])
    dst_prev = jnp.concatenate([before_first, dst[:R - tmm]])
    plan = jnp.concatenate([tile_expert, src[:tmm], dst[R - tmm:], tile_end[-1:]]).astype(jnp.int32)
    return plan, src_next, dst_prev, R + tmm


def _rope_tables(T, dk, tm):
    angle = 1.0 / (ROPE_BASE ** jnp.linspace(0.0, 1.0, dk // 2, dtype=F32))
    angle = jnp.repeat(angle, 2)
    sign = jnp.where(jnp.arange(dk) % 2 == 0, -1.0, 1.0).astype(F32)
    start = (jnp.arange(T // tm, dtype=F32) * tm)[:, None] * angle[None, :]
    off = jnp.arange(tm, dtype=F32)[:, None] * angle[None, :]
    return (jnp.cos(start)[:, None, :], (jnp.sin(start) * sign[None, :])[:, None, :],
            jnp.cos(off), jnp.sin(off) * sign[None, :])


def kernel(x, norm_mix, norm_ffn, norm_final, a_mu, a_wr, a_wk, a_wv, a_wo, a_w0, a_w1, a_w2, a_a0, a_a1, a_a2, a_g1, a_g2, a_kk, a_ka, a_rk, a_gn_w, a_gn_b, b_wq, b_wk, b_wv, b_wg, b_wo, f_w1, f_w3, f_w2, m_router, m_w1, m_w3, m_w2):
    B, T, D = x.shape
    N = B * T
    bf = lambda w: w.astype(BF16)
    vec = lambda w: w.reshape(1, -1).astype(F32)
    h = x.reshape(N, D)

    r, ld, k, v, a, g = _rwkv_proj(
        h, T, vec(norm_mix[0]), a_mu[0], bf(a_wr[0]), bf(a_wk[0]), bf(a_wv[0]), bf(a_w1[0]), bf(a_w2[0]),
        bf(a_a1[0]), bf(a_a2[0]), bf(a_g1[0]), bf(a_g2[0]), vec(a_w0[0]), vec(a_a0[0]))
    y = _rwkv_recur(r, ld, k, v, a, vec(a_kk[0]), vec(a_ka[0]), vec(a_rk[0]), vec(a_gn_w[0]),
                    vec(a_gn_b[0]), B, T)
    h = _mix_out_dense_ffn(h, y, g, bf(a_wo[0]), vec(norm_ffn[0]), bf(f_w1[0]), bf(f_w3[0]), bf(f_w2[0]))

    q, kr, vr, gr = _ret_proj(h, T, vec(norm_mix[1]), bf(b_wq[0]), bf(b_wk[0]), bf(b_wv[0]), bf(b_wg[0]))
    o = _ret_chunk(q, kr, vr, gr, B, T, RET_HEADS)
    h, u, idx, gate = _ret_out_router(h, o, bf(b_wo[0]), vec(norm_ffn[1]), m_router[0].T.astype(F32))
    plan, src_next, dst_prev, n_rows = _route_plan(idx, N)
    y = _moe_experts(u, plan, src_next, dst_prev, bf(m_w1[0]), bf(m_w3[0]), bf(m_w2[0]), n_rows)
    out = _combine(h, y, gate.T, vec(norm_final))
    return out.reshape(B, T, D)
```

```python
import functools
import math

import jax
import jax.numpy as jnp
from jax import lax
from jax.experimental import pallas as pl
from jax.experimental.pallas import tpu as pltpu

F32 = jnp.float32
BF16 = jnp.bfloat16

RMS_EPS = 1e-6
RWKV_HEAD = 64
RWKV_GN_EPS = 64e-5
RET_HEADS = 4
ROPE_BASE = 10000.0
N_EXPERTS = 8
TOP_K = 2

LANES = 128
VMEM_LIMIT = 56 * 1024 * 1024

RWKV_CHUNK = 64
RWKV_SUB = 16
RET_BLOCK = 256
MOE_TILE = 512
MOE_SUB = 256
FFN_SUB = 256
DMA_UNROLL = 8
WAIT_UNROLL = 64


def _cparams(sem):
    return pltpu.CompilerParams(dimension_semantics=sem, vmem_limit_bytes=VMEM_LIMIT)


def _dot(a, b):
    return jnp.dot(a.astype(BF16), b.astype(BF16), preferred_element_type=F32)


def _dot_nt(a, b):
    return lax.dot_general(a.astype(BF16), b.astype(BF16), (((1,), (1,)), ((), ())),
                           preferred_element_type=F32)


def _dot_tn(a, b):
    return lax.dot_general(a.astype(BF16), b.astype(BF16), (((0,), (0,)), ((), ())),
                           preferred_element_type=F32)


def _rms(x, g):
    return x * lax.rsqrt(jnp.mean(x * x, axis=-1, keepdims=True) + RMS_EPS) * g


def _sigmoid(x):
    return 1.0 / (1.0 + jnp.exp(-x))


def _silu(x):
    return x * _sigmoid(x)


def _full(shape):
    n = len(shape)
    return pl.BlockSpec(shape, lambda *_: (0,) * n, pipeline_mode=pl.Buffered(1))


def _rwkv_proj_kernel(T, tm, h_ref, hp_ref, nw_ref, mu_ref, wr_ref, wk_ref, wv_ref, w1_ref, w2_ref,
                      a1_ref, a2_ref, g1_ref, g2_ref, w0_ref, a0_ref,
                      r_out, ld_out, k_out, v_out, a_out, g_out):
    i = pl.program_id(0)
    nw = nw_ref[...]
    u = _rms(h_ref[...], nw)
    up = _rms(hp_ref[...], nw)
    first = (i * tm) % T == 0
    prev_last = jnp.where(first, 0.0, up[7:8, :])
    row = lax.broadcasted_iota(jnp.int32, u.shape, 0)
    us = jnp.where(row == 0, prev_last, pltpu.roll(u, 1, 0))
    dx = us - u

    def mix(j):
        return (u + dx * mu_ref[j:j + 1, :]).astype(BF16)

    r_out[...] = _dot(mix(0), wr_ref[...]).astype(BF16)
    z =w0_ref[...] + _dot(jnp.tanh(_dot(mix(1), w1_ref[...])), w2_ref[...])
    nz = -z
    softplus = jnp.maximum(nz, 0.0) + jnp.log(1.0 + jnp.exp(-jnp.abs(nz)))
    ld_out[...] = -jnp.exp(-softplus - 0.5)
    k_out[...] = _dot(mix(2), wk_ref[...]).astype(BF16)
    v_out[...] = _dot(mix(3), wv_ref[...]).astype(BF16)
    a_out[...] = _sigmoid(a0_ref[...] + _dot(_dot(mix(4), a1_ref[...]), a2_ref[...])).astype(BF16)
    g_out[...] = _dot(_sigmoid(_dot(mix(5), g1_ref[...])), g2_ref[...]).astype(BF16)


def _rwkv_proj(h, T, nw, mu, wr, wk, wv, w1, w2, a1, a2, g1, g2, w0, a0, tm=512):
    N, D = h.shape
    tm = min(tm, T)
    row = pl.BlockSpec((tm, D), lambda i: (i, 0))
    prev = pl.BlockSpec((8, D), lambda i: (jnp.maximum(i * (tm // 8) - 1, 0), 0))
    ws = [nw, mu, wr, wk, wv, w1, w2, a1, a2, g1, g2, w0, a0]
    out = lambda dt: jax.ShapeDtypeStruct((N, D), dt)
    return pl.pallas_call(
        functools.partial(_rwkv_proj_kernel, T, tm),
        grid=(N // tm,),
        in_specs=[row, prev] + [_full(w.shape) for w in ws],
        out_specs=[row] * 6,
        out_shape=[out(BF16), out(F32), out(BF16), out(BF16), out(BF16), out(BF16)],
        compiler_params=_cparams(("parallel",)),
        name="rwkv_proj",
    )(h, h, *ws)


def _rwkv_pair_chunk(r, ld, cum, k, v, a, kkp, kap, rkp, gnw, gnb, s_ref):
    L = r.shape[0]
    L2 = 2 * L
    lane = lax.broadcasted_iota(jnp.int32, (L, LANES), 1)
    m0 = lane < RWKV_HEAD

    def seg_sum(x):
        s0 = jnp.sum(jnp.where(m0, x, 0.0), axis=-1, keepdims=True)
        s1 = jnp.sum(jnp.where(m0, 0.0, x), axis=-1, keepdims=True)
        return jnp.where(m0, s0, s1)

    def stack(x):
        return jnp.concatenate([jnp.where(m0, x, 0.0), jnp.where(m0, 0.0, x)], axis=0)

    kk = k * kkp
    kk = kk / jnp.maximum(jnp.sqrt(seg_sum(kk * kk)), 1e-12)
    kmod = k * (1.0 + (a - 1.0) * kap)
    aa = -kk
    bb = kk * a

    tot = cum[L - 1:L, :]
    e_cum = jnp.exp(cum)
    e_inv = jnp.exp(-cum)
    e_fin = jnp.exp(tot - cum)
    a_t = stack(aa * jnp.exp(cum - ld))
    r_t = stack(r * e_cum)
    b_t = stack(bb * e_inv)
    k_t = stack(kmod * e_inv)
    b_h = stack(bb * e_fin)
    k_h = stack(kmod * e_fin)
    v_s = stack(v)

    mm = _dot_nt(jnp.concatenate([a_t, r_t], axis=0), jnp.concatenate([b_t, k_t], axis=0))
    yield
    si =lax.broadcasted_iota(jnp.int32, (L2, L2), 0)
    sj = lax.broadcasted_iota(jnp.int32, (L2, L2), 1)
    low = sj < si
    lowi = sj <= si
    m_ab = jnp.where(low, mm[:L2, :L2], 0.0)
    m_ak = jnp.where(low, mm[:L2, L2:], 0.0)
    m_rb = jnp.where(lowi, mm[L2:, :L2], 0.0)
    m_rk = jnp.where(lowi, mm[L2:, L2:], 0.0)

    eye = jnp.where(si == sj, 1.0, 0.0).astype(F32)
    sub_shift = RWKV_SUB.bit_length() - 1
    same = (si >> sub_shift) == (sj >> sub_shift)
    dg = jnp.where(same, m_ab, 0.0)
    og = jnp.where(same, 0.0, m_ab)
    d2 = _dot(dg, dg)
    s0 = s_ref[...]
    ar_s = _dot_nt(jnp.concatenate([a_t, r_t], axis=0), s0)
    c = ar_s[:L2] + _dot(m_ak, v_s)
    yield
    d4 = _dot(d2, d2)
    x1 = eye + dg + d2 + _dot(dg, d2)
    yield
    d8 = _dot(d4, d4)
    yield
    x2 = eye + d4 + d8 + _dot(d4, d8)
    yield
    t_d = _dot(x1, x2)
    yield
    e1 = _dot(t_d, og)
    yield
    e2 = _dot(e1, e1)
    yield
    t_f = eye + e1 + e2 + _dot(e1, e2)
    yield
    t_m = _dot(t_f, t_d)
    yield
    u = _dot(t_m, c)
    yield
    uv = jnp.concatenate([u, v_s], axis=0)
    ys = ar_s[L2:] + _dot(jnp.concatenate([m_rb, m_rk], axis=1), uv)
    s_ref[...] = s0 * jnp.exp(tot) + _dot_tn(uv, jnp.concatenate([b_h, k_h], axis=0))
    yield
    y = ys[:L] + ys[L:]

    inv_n = 1.0 / RWKV_HEAD
    mean = seg_sum(y) * inv_n
    yc = y - mean
    var = seg_sum(yc * yc) * inv_n
    yn = yc * lax.rsqrt(var + RWKV_GN_EPS) * gnw + gnb
    return yn + seg_sum(r * kmod * rkp) * v


def _run_in_lockstep(gens):
    results = [None] * len(gens)
    live = list(range(len(gens)))
    while live:
        still = []
        for i in live:
            try:
                next(gens[i])
                still.append(i)
            except StopIteration as done:
                results[i] = done.value
        live = still
    return results


def _rwkv_recur_kernel(G, L, r_ref, ld_ref, k_ref, v_ref, a_ref, kk_ref, ka_ref, rk_ref, gw_ref, gb_ref,
                       y_ref, s_ref):
    @pl.when(pl.program_id(0) == 0)
    def _():
        s_ref[...] = jnp.zeros_like(s_ref)

    B = r_ref.shape[0]
    blocks = [(b, slice(g * LANES, (g + 1) * LANES)) for b in range(B) for g in range(G)]
    ti = lax.broadcasted_iota(jnp.int32, (L, L), 0)
    tj = lax.broadcasted_iota(jnp.int32, (L, L), 1)
    tri = jnp.where(tj <= ti, 1.0, 0.0).astype(BF16)

    def chunk(c, carry):
        rows = pl.ds(pl.multiple_of(c * L, L), L)
        lds, cums = [], []
        for b in range(B):
            ld = ld_ref[b, rows, :]
            cum = jnp.zeros_like(ld)
            rest = ld
            for _ in range(3):
                part = rest.astype(BF16)
                cum = cum + jnp.dot(tri, part, preferred_element_type=F32)
                rest = rest - part.astype(F32)
            lds.append(ld)
            cums.append(cum)
        f32 = lambda ref, b, sl: ref[b, rows, sl].astype(F32)
        ys = _run_in_lockstep([
            _rwkv_pair_chunk(f32(r_ref, b, sl), lds[b][:, sl], cums[b][:, sl], f32(k_ref, b, sl),
                             f32(v_ref, b, sl), f32(a_ref, b, sl), kk_ref[:, sl], ka_ref[:, sl],
                             rk_ref[:, sl], gw_ref[:, sl], gb_ref[:, sl], s_ref.at[n])
            for n, (b, sl) in enumerate(blocks)])
        for (b, sl), y in zip(blocks, ys):
            y_ref[b, rows, sl] = y.astype(BF16)
        return carry

    lax.fori_loop(0, r_ref.shape[1] // L, chunk, 0)


def _rwkv_recur(r, ld, k, v, a, kkp, kap, rkp, gnw, gnb, B, T, chunks_per_step=4):
    N, D = r.shape
    L = RWKV_CHUNK
    G = D // LANES
    rows = L * chunks_per_step
    seq = pl.BlockSpec((B, rows, D), lambda c: (0, c, 0))
    par = pl.BlockSpec((1, D), lambda c: (0, 0))
    b3 = lambda t: t.reshape(B, T, D)
    y = pl.pallas_call(
        functools.partial(_rwkv_recur_kernel, G, L),
        grid=(T // rows,),
        in_specs=[seq] * 5 + [par] * 5,
        out_specs=seq,
        out_shape=jax.ShapeDtypeStruct((B, T, D), BF16),
        scratch_shapes=[pltpu.VMEM((B * G, LANES, LANES), F32)],
        compiler_params=_cparams(("arbitrary",)),
        name="rwkv_recur",
    )(b3(r), b3(ld), b3(k), b3(v), b3(a), kkp, kap, rkp, gnw, gnb)
    return y.reshape(N, D)


def _swiglu_chunk(u, w1, w3, w2):
    return _dot(_silu(_dot(u, w1)) * _dot(u, w3), w2)


def _dense_ffn_kernel(h_ref, y_ref, g_ref, wo_ref, nw_ref, w1_ref, w3_ref, w2_ref, o_ref, u_ref):
    h = h_ref[...] + _dot(y_ref[...].astype(F32) * g_ref[...].astype(F32), wo_ref[...])
    u_ref[...] = _rms(h, nw_ref[...]).astype(BF16)
    o_ref[...] = h
    for c in range(w1_ref.shape[1] // FFN_SUB):
        cs = slice(c * FFN_SUB, (c + 1) * FFN_SUB)
        o_ref[...] += _swiglu_chunk(u_ref[...], w1_ref[:, cs], w3_ref[:, cs], w2_ref[cs, :])


def _mix_out_dense_ffn(h, y, g, wo, nw, w1, w3, w2, tm=512):
    N, D = h.shape
    tm = min(tm, N)
    row = pl.BlockSpec((tm, D), lambda i: (i, 0))
    return pl.pallas_call(
        _dense_ffn_kernel,
        grid=(N // tm,),
        in_specs=[row, row, row] + [_full(w.shape) for w in (wo, nw, w1, w3, w2)],
        out_specs=row,
        out_shape=jax.ShapeDtypeStruct((N, D), F32),
        scratch_shapes=[pltpu.VMEM((tm, D), BF16)],
        compiler_params=_cparams(("parallel",)),
        name="dense_ffn",
    )(h, y, g, wo, nw, w1, w3, w2)


def _ret_proj_kernel(dk, h_ref, nw_ref, cs_ref, ss_ref, co_ref, so_ref, wq_ref, wk_ref, wv_ref, wg_ref,
                     q_out, k_out, v_out, g_out):
    u = _rms(h_ref[...], nw_ref[...]).astype(BF16)
    nh = wq_ref.shape[1] // dk
    cs, ss, co, so = cs_ref[0], ss_ref[0], co_ref[...], so_ref[...]
    cos = jnp.concatenate([cs * co - ss * so] * nh, axis=1)
    sin = jnp.concatenate([ss * co + cs * so] * nh, axis=1)
    lane = lax.broadcasted_iota(jnp.int32, cos.shape, 1)
    even = (lane & 1) == 0
    width = cos.shape[1]

    def rope(t):
        nxt = pltpu.roll(t, width - 1, 1)
        prv = pltpu.roll(t, 1, 1)
        return t * cos + jnp.where(even, nxt, prv) * sin

    q_out[...] = rope(_dot(u, wq_ref[...])).astype(BF16)
    k_out[...] = rope(_dot(u, wk_ref[...]) * (dk ** -0.5)).astype(BF16)
    v_out[...] = _dot(u, wv_ref[...]).astype(BF16)
    g_out[...] = _dot(u, wg_ref[...]).astype(BF16)


def _ret_proj(h, T, nw, wq, wk, wv, wg, tm=512):
    N, D = h.shape
    dk = wq.shape[1] // RET_HEADS
    tm = min(tm, T)
    tpb = T // tm
    cs, ss, co, so = _rope_tables(T, dk, tm)
    row = lambda w: pl.BlockSpec((tm, w), lambda i: (i, 0))
    start = pl.BlockSpec((1, 1, dk), lambda i: (i % tpb, 0, 0))
    return pl.pallas_call(
        functools.partial(_ret_proj_kernel, dk),
        grid=(N // tm,),
        in_specs=[row(D), _full(nw.shape), start, start, _full(co.shape), _full(so.shape)]
        + [_full(w.shape) for w in (wq, wk, wv, wg)],
        out_specs=[row(wq.shape[1]), row(wk.shape[1]), row(wv.shape[1]), row(wg.shape[1])],
        out_shape=[jax.ShapeDtypeStruct((N, w.shape[1]), BF16) for w in (wq, wk, wv, wg)],
        compiler_params=_cparams(("parallel",)),
        name="ret_proj",
    )(h, nw, cs, ss, co, so, wq, wk, wv, wg)


def _ret_head_chunk(q, k, v, g, intra, qd, kd, cd, r_ref):
    r0 = r_ref[...]
    s = _dot_nt(q, k) * intra
    inter = _dot(q, r0) * qd
    r_ref[...] = r0 * cd + _dot_tn(k.astype(F32) * kd, v)
    yield
    o = _dot(s, v) + inter
    yield
    o = o * lax.rsqrt(jnp.mean(o * o, axis=-1, keepdims=True) + RMS_EPS)
    return (_silu(g.astype(F32)) * o).astype(BF16)


def _ret_chunk_kernel(H, q_ref, k_ref, v_ref, g_ref, intra_ref, qd_ref, kd_ref, cd_ref, o_ref, r_ref):
    @pl.when(pl.program_id(1) == 0)
    def _():
        r_ref[...] = jnp.zeros_like(r_ref)

    dk = q_ref.shape[1] // H
    dv = v_ref.shape[1] // H
    ks = [slice(h * dk, (h + 1) * dk) for h in range(H)]
    vs = [slice(h * dv, (h + 1) * dv) for h in range(H)]
    outs = _run_in_lockstep([
        _ret_head_chunk(q_ref[:, ks[h]], k_ref[:, ks[h]], v_ref[:, vs[h]], g_ref[:, vs[h]],
                        intra_ref[h], qd_ref[h], kd_ref[h], cd_ref[h], r_ref.at[h])
        for h in range(H)])
    for h in range(H):
        o_ref[:, vs[h]] = outs[h]


def _ret_chunk(q, k, v, g, B, T, H):
    N = q.shape[0]
    dk = q.shape[1] // H
    dv = v.shape[1] // H
    C = min(RET_BLOCK, T)
    nc = T // C
    lg = jnp.log(1.0 - 2.0 ** (-5.0 - jnp.arange(H, dtype=F32)))
    n = jnp.arange(C, dtype=F32)
    diff = n[:, None] - n[None, :]
    intra = jnp.where(diff >= 0, jnp.exp(lg[:, None, None] * jnp.maximum(diff, 0.0)), 0.0)
    qd = jnp.broadcast_to(jnp.exp(lg[:, None] * (n[None, :] + 1.0))[:, :, None], (H, C, LANES))
    kd = jnp.broadcast_to(jnp.exp(lg[:, None] * (C - 1.0 - n[None, :]))[:, :, None], (H, C, LANES))
    cd = jnp.broadcast_to(jnp.exp(lg * C)[:, None, None], (H, 8, LANES))
    qd = jnp.concatenate([qd] * (dv // LANES), axis=2)
    kd = jnp.concatenate([kd] * (dk // LANES), axis=2)
    cd = jnp.concatenate([cd] * (dv // LANES), axis=2)[:, :1]
    seq = lambda w: pl.BlockSpec((C, w), lambda b, c: (b * nc + c, 0))
    return pl.pallas_call(
        functools.partial(_ret_chunk_kernel, H),
        grid=(B, nc),
        in_specs=[seq(H * dk), seq(H * dk), seq(H * dv), seq(H * dv)]
        + [_full(t.shape) for t in (intra, qd, kd, cd)],
        out_specs=seq(H * dv),
        out_shape=jax.ShapeDtypeStruct((N, H * dv), BF16),
        scratch_shapes=[pltpu.VMEM((H, dk, dv), F32)],
        compiler_params=_cparams(("parallel", "arbitrary")),
        name="ret_chunk",
    )(q, k, v, g, intra, qd, kd, cd)


def _ret_out_router_kernel(h_ref, o_ref, wo_ref, nw_ref, rt_ref, h_out, u_out, idx_out, gate_out):
    h = h_ref[...] + _dot(o_ref[...], wo_ref[...])
    h_out[...] = h
    u = _rms(h, nw_ref[...])
    u_out[...] = u
    logits = lax.dot_general(rt_ref[...], u, (((1,), (1,)), ((), ())),
                             precision=lax.Precision.HIGHEST, preferred_element_type=F32)
    e_iota = lax.broadcasted_iota(jnp.int32, logits.shape, 0)
    big = jnp.int32(N_EXPERTS)
    m1 = jnp.max(logits, axis=0, keepdims=True)
    i1 = jnp.min(jnp.where(logits == m1, e_iota, big), axis=0, keepdims=True)
    rest = jnp.where(e_iota == i1, -jnp.inf, logits)
    m2 = jnp.max(rest, axis=0, keepdims=True)
    i2 = jnp.min(jnp.where(rest == m2, e_iota, big), axis=0, keepdims=True)
    p2 = jnp.exp(m2 - m1)
    g1 = 1.0 / (1.0 + p2)
    idx_out[...] = jnp.concatenate([i1, i2], axis=0)
    gate_out[...] = jnp.concatenate([g1, p2 * g1], axis=0)


def _ret_out_router(h, o, wo, nw, router_t, tm=512):
    N, D = h.shape
    K = o.shape[1]
    tm = min(tm, N)
    row = lambda w: pl.BlockSpec((tm, w), lambda i: (i, 0))
    col = pl.BlockSpec((TOP_K, tm), lambda i: (0, i))
    return pl.pallas_call(
        _ret_out_router_kernel,
        grid=(N // tm,),
        in_specs=[row(D), row(K), _full(wo.shape), _full(nw.shape), _full(router_t.shape)],
        out_specs=[row(D), row(D), col, col],
        out_shape=[jax.ShapeDtypeStruct((N, D), F32), jax.ShapeDtypeStruct((N, D), F32),
                   jax.ShapeDtypeStruct((TOP_K, N), jnp.int32), jax.ShapeDtypeStruct((TOP_K, N), F32)],
        compiler_params=_cparams(("parallel",)),
        name="ret_out_router",
    )(h, o, wo, nw, router_t)


def _row_copy(src_hbm, dst_ref, sem, s, j):
    return pltpu.make_async_copy(src_hbm.at[pl.ds(s, 1), :], dst_ref.at[pl.ds(j, 1), :], sem)


def _moe_kernel(tmm, q, te_ref, src_ref, dst_ref, u_hbm, w1_ref, w3_ref, w2_ref, y_hbm,
                xbuf, xb_ref, obuf, gsem, ssem):
    i = pl.program_id(0)
    f = pl.program_id(1)
    n_i = pl.num_programs(0)
    n_f = pl.num_programs(1)
    cur = i % 2
    oth = 1 - cur

    def gather(slot, j, tok):
        return _row_copy(u_hbm, xbuf.at[slot], gsem.at[slot], tok, j)

    def scatter(slot, j, row):
        return pltpu.make_async_copy(obuf.at[slot].at[pl.ds(j, 1), :], y_hbm.at[pl.ds(row, 1), :],
                                     ssem.at[slot])

    def wait_rows(make):
        def body(j, c):
            make(j).wait()
            return c
        lax.fori_loop(0, tmm, body, 0, unroll=WAIT_UNROLL)

    @pl.when((i == 0) & (f == 0))
    def _():
        obuf[1] = jnp.zeros((tmm, obuf.shape[2]), F32)

        def first(j, c):
            gather(0, j, te_ref[n_i + j]).start()
            return c
        lax.fori_loop(0, tmm, first, 0, unroll=DMA_UNROLL)

    @pl.when(f == 0)
    def _():
        wait_rows(lambda j: gather(cur, j, 0))
        xb_ref[...] = xbuf[cur].astype(BF16)

        @pl.when(i > 0)
        def _():
            wait_rows(lambda j: scatter(cur, j, 0))
        obuf[cur] = jnp.zeros((tmm, obuf.shape[2]), F32)

    def issue_rows(slot, first):
        for j in range(first, first + q):
            gather(slot, j, src_ref[j]).start()
            scatter(slot, j, dst_ref[j]).start(priority=j % 2)

    for slot in range(2):
        for ff in range(tmm // q):
            pl.when((oth == slot) & (f == ff))(functools.partial(issue_rows, slot, ff * q))

    @pl.when(i < te_ref[n_i + 2 * tmm])
    def _():
        for c in range(w1_ref.shape[2] // MOE_SUB):
            cs = slice(c * MOE_SUB, (c + 1) * MOE_SUB)
            obuf[cur] += _swiglu_chunk(xb_ref[...], w1_ref[0, :, cs], w3_ref[0, :, cs], w2_ref[0, cs, :])

    @pl.when((f == n_f - 1) & (i == n_i - 1))
    def _():
        def last(j, c):
            scatter(cur, j, te_ref[n_i + tmm + j]).start()
            return c
        lax.fori_loop(0, tmm, last, 0, unroll=DMA_UNROLL)
        wait_rows(lambda j: scatter(oth, j, 0))
        wait_rows(lambda j: scatter(cur, j, 0))
        wait_rows(lambda j: gather(oth, j, 0))


def _moe_experts(u, plan, src_next, dst_prev, w1, w3, w2, n_rows_out, tf=1792):
    N, D = u.shape
    Fd = w1.shape[2]
    tmm = MOE_TILE
    n_tiles = src_next.shape[0] // tmm
    nf = Fd // tf
    q = tmm // nf
    smem = lambda: pl.BlockSpec((tmm,), lambda i, f, te: (i,), memory_space=pltpu.SMEM)
    gs = pltpu.PrefetchScalarGridSpec(
        num_scalar_prefetch=1,
        grid=(n_tiles, nf),
        in_specs=[smem(), smem(), pl.BlockSpec(memory_space=pl.ANY),
                  pl.BlockSpec((1, D, tf), lambda i, f, te: (te[i], 0, f)),
                  pl.BlockSpec((1, D, tf), lambda i, f, te: (te[i], 0, f)),
                  pl.BlockSpec((1, tf, D), lambda i, f, te: (te[i], f, 0))],
        out_specs=pl.BlockSpec(memory_space=pl.ANY),
        scratch_shapes=[pltpu.VMEM((2, tmm, D), F32), pltpu.VMEM((tmm, D), BF16),
                        pltpu.VMEM((2, tmm, D), F32), pltpu.SemaphoreType.DMA((2,)),
                        pltpu.SemaphoreType.DMA((2,))],
    )
    return pl.pallas_call(
        functools.partial(_moe_kernel, tmm, q),
        grid_spec=gs,
        out_shape=jax.ShapeDtypeStruct((n_rows_out, D), F32),
        compiler_params=_cparams(("arbitrary", "arbitrary")),
        name="moe_experts",
    )(plan, src_next, dst_prev, u, w1, w3, w2)


def _combine_kernel(h_ref, y0_ref, y1_ref, gate_ref, nw_ref, o_ref):
    gate = gate_ref[...]
    moe = gate[:, 0:1] * y0_ref[...] + gate[:, 1:2] * y1_ref[...]
    o_ref[...] = _rms(h_ref[...] + moe, nw_ref[...])


def _combine(h, y, gate, nw, tc=512):
    N, D = h.shape
    tc = min(tc, N)
    nb = N // tc
    return pl.pallas_call(
        _combine_kernel,
        grid=(nb,),
        in_specs=[pl.BlockSpec((tc, D), lambda i: (i, 0)),
                  pl.BlockSpec((tc, D), lambda i: (i, 0)),
                  pl.BlockSpec((tc, D), lambda i: (nb + i, 0)),
                  pl.BlockSpec((tc, TOP_K), lambda i: (i, 0)),
                  _full(nw.shape)],
        out_specs=pl.BlockSpec((tc, D), lambda i: (i, 0)),
        out_shape=jax.ShapeDtypeStruct((N, D), F32),
        compiler_params=_cparams(("parallel",)),
        name="moe_combine",
    )(h, y, y, gate, nw)


def _route_plan(idx, N):
    tmm = MOE_TILE
    n_asg = TOP_K * N
    n_tiles = n_asg // tmm + N_EXPERTS
    R = n_tiles * tmm
    flat_e = idx.reshape(-1)
    asg = jnp.arange(n_asg, dtype=jnp.int32)
    order = jnp.sort(flat_e * n_asg + asg) % n_asg
    counts = jnp.sum((flat_e[None, :] == jnp.arange(N_EXPERTS, dtype=jnp.int32)[:, None]).astype(jnp.int32),
                     axis=1)
    ptiles = (counts + tmm - 1) // tmm
    tile_end = jnp.cumsum(ptiles)
    tile_start = tile_end - ptiles
    cum_incl = jnp.cumsum(counts)
    cum_excl = cum_incl - counts
    t = jnp.arange(n_tiles, dtype=jnp.int32)
    tile_e = jnp.sum((t[:, None] >= tile_end[None, :]).astype(jnp.int32), axis=1)
    tile_expert = jnp.minimum(tile_e, N_EXPERTS - 1)
    span = n_asg + tmm
    order_pad = jnp.concatenate([order, jnp.zeros((span,), jnp.int32)])
    rows = jnp.zeros((R + span,), jnp.int32)
    for e in range(N_EXPERTS):
        run = lax.dynamic_slice(order_pad, (cum_excl[e],), (span,))
        rows = lax.dynamic_update_slice(rows, run, (tile_start[e] * tmm,))
    k = t - tile_start[tile_expert]
    n_real = jnp.where(tile_e < N_EXPERTS, jnp.clip(counts[tile_expert] - k * tmm, 0, tmm), 0)
    j = jnp.arange(tmm, dtype=jnp.int32)
    real_before = jnp.concatenate([cum_incl, jnp.full((1,), n_asg, jnp.int32)])[tile_e]
    spare = n_asg + (t * tmm - real_before)[:, None] + j[None, :]
    dst = jnp.where(j[None, :] < n_real[:, None], rows[:R].reshape(n_tiles, tmm), spare).reshape(R)
    src = jnp.where(dst < n_asg, dst % N, 0)
    zeros = jnp.zeros((tmm,), jnp.int32)
    before_first = R + jnp.arange(tmm, dtype=jnp.int32)
    src_next = jnp.concatenate([src[tmm:], zeros])
    dst_prev = jnp.concatenate([before_first, dst[:R - tmm]])
    plan = jnp.concatenate([tile_expert, src[:tmm], dst[R - tmm:], tile_end[-1:]]).astype(jnp.int32)
    return plan, src_next, dst_prev, R + tmm


def _rope_tables(T, dk, tm):
    angle = 1.0 / (ROPE_BASE ** jnp.linspace(0.0, 1.0, dk // 2, dtype=F32))
    angle = jnp.repeat(angle, 2)
    sign = jnp.where(jnp.arange(dk) % 2 == 0, -1.0, 1.0).astype(F32)
    start = (jnp.arange(T // tm, dtype=F32) * tm)[:, None] * angle[None, :]
    off = jnp.arange(tm, dtype=F32)[:, None] * angle[None, :]
    return (jnp.cos(start)[:, None, :], (jnp.sin(start) * sign[None, :])[:, None, :],
            jnp.cos(off), jnp.sin(off) * sign[None, :])


def kernel(x, norm_mix, norm_ffn, norm_final, a_mu, a_wr, a_wk, a_wv, a_wo, a_w0, a_w1, a_w2, a_a0, a_a1, a_a2, a_g1, a_g2, a_kk, a_ka, a_rk, a_gn_w, a_gn_b, b_wq, b_wk, b_wv, b_wg, b_wo, f_w1, f_w3, f_w2, m_router, m_w1, m_w3, m_w2):
    B, T, D = x.shape
    N = B * T
    bf = lambda w: w.astype(BF16)
    vec = lambda w: w.reshape(1, -1).astype(F32)
    h = x.reshape(N, D)

    r, ld, k, v, a, g = _rwkv_proj(
        h, T, vec(norm_mix[0]), a_mu[0], bf(a_wr[0]), bf(a_wk[0]), bf(a_wv[0]), bf(a_w1[0]), bf(a_w2[0]),
        bf(a_a1[0]), bf(a_a2[0]), bf(a_g1[0]), bf(a_g2[0]), vec(a_w0[0]), vec(a_a0[0]))
    y = _rwkv_recur(r, ld, k, v, a, vec(a_kk[0]), vec(a_ka[0]), vec(a_rk[0]), vec(a_gn_w[0]),
                    vec(a_gn_b[0]), B, T)
    h = _mix_out_dense_ffn(h, y, g, bf(a_wo[0]), vec(norm_ffn[0]), bf(f_w1[0]), bf(f_w3[0]), bf(f_w2[0]))

    q, kr, vr, gr = _ret_proj(h, T, vec(norm_mix[1]), bf(b_wq[0]), bf(b_wk[0]), bf(b_wv[0]), bf(b_wg[0]))
    o = _ret_chunk(q, kr, vr, gr, B, T, RET_HEADS)
    h, u, idx, gate = _ret_out_router(h, o, bf(b_wo[0]), vec(norm_ffn[1]), m_router[0].T.astype(F32))
    plan, src_next, dst_prev, n_rows = _route_plan(idx, N)
    y = _moe_experts(u, plan, src_next, dst_prev, bf(m_w1[0]), bf(m_w3[0]), bf(m_w2[0]), n_rows)
    out = _combine(h, y, gate.T, vec(norm_final))
    return out.reshape(B, T, D)
```

```python
import functools
import math

import jax
import jax.numpy as jnp
from jax import lax
from jax.experimental import pallas as pl
from jax.experimental.pallas import tpu as pltpu

F32 = jnp.float32
BF16 = jnp.bfloat16

RMS_EPS = 1e-6
RWKV_HEAD = 64
RWKV_GN_EPS = 64e-5
RET_HEADS = 4
ROPE_BASE = 10000.0
N_EXPERTS = 8
TOP_K = 2

LANES = 128
VMEM_LIMIT = 56 * 1024 * 1024

RWKV_CHUNK = 64
RWKV_SUB = 16
RET_BLOCK = 256
MOE_TILE = 512
MOE_SUB = 256
FFN_SUB = 256
DMA_UNROLL = 8
WAIT_UNROLL = 64


def _cparams(sem):
    return pltpu.CompilerParams(dimension_semantics=sem, vmem_limit_bytes=VMEM_LIMIT)


def _dot(a, b):
    return jnp.dot(a.astype(BF16), b.astype(BF16), preferred_element_type=F32)


def _dot_nt(a, b):
    return lax.dot_general(a.astype(BF16), b.astype(BF16), (((1,), (1,)), ((), ())),
                           preferred_element_type=F32)


def _dot_tn(a, b):
    return lax.dot_general(a.astype(BF16), b.astype(BF16), (((0,), (0,)), ((), ())),
                           preferred_element_type=F32)


def _rms(x, g):
    return x * lax.rsqrt(jnp.mean(x * x, axis=-1, keepdims=True) + RMS_EPS) * g


def _sigmoid(x):
    return 1.0 / (1.0 + jnp.exp(-x))


def _silu(x):
    return x * _sigmoid(x)


def _full(shape):
    n = len(shape)
    return pl.BlockSpec(shape, lambda *_: (0,) * n, pipeline_mode=pl.Buffered(1))


def _rwkv_proj_kernel(T, tm, h_ref, hp_ref, nw_ref, mu_ref, wr_ref, wk_ref, wv_ref, w1_ref, w2_ref,
                      a1_ref, a2_ref, g1_ref, g2_ref, w0_ref, a0_ref,
                      r_out, ld_out, k_out, v_out, a_out, g_out):
    i = pl.program_id(0)
    nw = nw_ref[...]
    u = _rms(h_ref[...], nw)
    up = _rms(hp_ref[...], nw)
    first = (i * tm) % T == 0
    prev_last = jnp.where(first, 0.0, up[7:8, :])
    row = lax.broadcasted_iota(jnp.int32, u.shape, 0)
    us = jnp.where(row == 0, prev_last, pltpu.roll(u, 1, 0))
    dx = us - u

    def mix(j):
        return (u + dx * mu_ref[j:j + 1, :]).astype(BF16)

    r_out[...] = _dot(mix(0), wr_ref[...]).astype(BF16)
    z =w0_ref[...] + _dot(jnp.tanh(_dot(mix(1), w1_ref[...])), w2_ref[...])
    nz = -z
    softplus = jnp.maximum(nz, 0.0) + jnp.log(1.0 + jnp.exp(-jnp.abs(nz)))
    ld_out[...] = -jnp.exp(-softplus - 0.5)
    k_out[...] = _dot(mix(2), wk_ref[...]).astype(BF16)
    v_out[...] = _dot(mix(3), wv_ref[...]).astype(BF16)
    a_out[...] = _sigmoid(a0_ref[...] + _dot(_dot(mix(4), a1_ref[...]), a2_ref[...])).astype(BF16)
    g_out[...] = _dot(_sigmoid(_dot(mix(5), g1_ref[...])), g2_ref[...]).astype(BF16)


def _rwkv_proj(h, T, nw, mu, wr, wk, wv, w1, w2, a1, a2, g1, g2, w0, a0, tm=512):
    N, D = h.shape
    tm = min(tm, T)
    row = pl.BlockSpec((tm, D), lambda i: (i, 0))
    prev = pl.BlockSpec((8, D), lambda i: (jnp.maximum(i * (tm // 8) - 1, 0), 0))
    ws = [nw, mu, wr, wk, wv, w1, w2, a1, a2, g1, g2, w0, a0]
    out = lambda dt: jax.ShapeDtypeStruct((N, D), dt)
    return pl.pallas_call(
        functools.partial(_rwkv_proj_kernel, T, tm),
        grid=(N // tm,),
        in_specs=[row, prev] + [_full(w.shape) for w in ws],
        out_specs=[row] * 6,
        out_shape=[out(BF16), out(F32), out(BF16), out(BF16), out(BF16), out(BF16)],
        compiler_params=_cparams(("parallel",)),
        name="rwkv_proj",
    )(h, h, *ws)


def _rwkv_pair_chunk(r, ld, cum, k, v, a, kkp, kap, rkp, gnw, gnb, s_ref):
    L = r.shape[0]
    L2 = 2 * L
    lane = lax.broadcasted_iota(jnp.int32, (L, LANES), 1)
    m0 = lane < RWKV_HEAD

    def seg_sum(x):
        s0 = jnp.sum(jnp.where(m0, x, 0.0), axis=-1, keepdims=True)
        s1 = jnp.sum(jnp.where(m0, 0.0, x), axis=-1, keepdims=True)
        return jnp.where(m0, s0, s1)

    def stack(x):
        return jnp.concatenate([jnp.where(m0, x, 0.0), jnp.where(m0, 0.0, x)], axis=0)

    kk = k * kkp
    kk = kk / jnp.maximum(jnp.sqrt(seg_sum(kk * kk)), 1e-12)
    kmod = k * (1.0 + (a - 1.0) * kap)
    aa = -kk
    bb = kk * a

    tot = cum[L - 1:L, :]
    e_cum = jnp.exp(cum)
    e_inv = jnp.exp(-cum)
    e_fin = jnp.exp(tot - cum)
    a_t = stack(aa * jnp.exp(cum - ld))
    r_t = stack(r * e_cum)
    b_t = stack(bb * e_inv)
    k_t = stack(kmod * e_inv)
    b_h = stack(bb * e_fin)
    k_h = stack(kmod * e_fin)
    v_s = stack(v)

    mm = _dot_nt(jnp.concatenate([a_t, r_t], axis=0), jnp.concatenate([b_t, k_t], axis=0))
    yield
    si =lax.broadcasted_iota(jnp.int32, (L2, L2), 0)
    sj = lax.broadcasted_iota(jnp.int32, (L2, L2), 1)
    low = sj < si
    lowi = sj <= si
    m_ab = jnp.where(low, mm[:L2, :L2], 0.0)
    m_ak = jnp.where(low, mm[:L2, L2:], 0.0)
    m_rb = jnp.where(lowi, mm[L2:, :L2], 0.0)
    m_rk = jnp.where(lowi, mm[L2:, L2:], 0.0)

    eye = jnp.where(si == sj, 1.0, 0.0).astype(F32)
    sub_shift = RWKV_SUB.bit_length() - 1
    same = (si >> sub_shift) == (sj >> sub_shift)
    dg = jnp.where(same, m_ab, 0.0)
    og = jnp.where(same, 0.0, m_ab)
    d2 = _dot(dg, dg)
    s0 = s_ref[...]
    ar_s = _dot_nt(jnp.concatenate([a_t, r_t], axis=0), s0)
    c = ar_s[:L2] + _dot(m_ak, v_s)
    yield
    d4 = _dot(d2, d2)
    x1 = eye + dg + d2 + _dot(dg, d2)
    yield
    d8 = _dot(d4, d4)
    yield
    x2 = eye + d4 + d8 + _dot(d4, d8)
    yield
    t_d = _dot(x1, x2)
    yield
    e1 = _dot(t_d, og)
    yield
    e2 = _dot(e1, e1)
    yield
    t_f = eye + e1 + e2 + _dot(e1, e2)
    yield
    t_m = _dot(t_f, t_d)
    yield
    u = _dot(t_m, c)
    yield
    uv = jnp.concatenate([u, v_s], axis=0)
    ys = ar_s[L2:] + _dot(jnp.concatenate([m_rb, m_rk], axis=1), uv)
    s_ref[...] = s0 * jnp.exp(tot) + _dot_tn(uv, jnp.concatenate([b_h, k_h], axis=0))
    yield
    y = ys[:L] + ys[L:]

    inv_n = 1.0 / RWKV_HEAD
    mean = seg_sum(y) * inv_n
    yc = y - mean
    var = seg_sum(yc * yc) * inv_n
    yn = yc * lax.rsqrt(var + RWKV_GN_EPS) * gnw + gnb
    return yn + seg_sum(r * kmod * rkp) * v


def _run_in_lockstep(gens):
    results = [None] * len(gens)
    live = list(range(len(gens)))
    while live:
        still = []
        for i in live:
            try:
                next(gens[i])
                still.append(i)
            except StopIteration as done:
                results[i] = done.value
        live = still
    return results


def _rwkv_recur_kernel(G, L, r_ref, ld_ref, k_ref, v_ref, a_ref, kk_ref, ka_ref, rk_ref, gw_ref, gb_ref,
                       y_ref, s_ref):
    @pl.when(pl.program_id(0) == 0)
    def _():
        s_ref[...] = jnp.zeros_like(s_ref)

    B = r_ref.shape[0]
    blocks = [(b, slice(g * LANES, (g + 1) * LANES)) for b in range(B) for g in range(G)]
    ti = lax.broadcasted_iota(jnp.int32, (L, L), 0)
    tj = lax.broadcasted_iota(jnp.int32, (L, L), 1)
    tri = jnp.where(tj <= ti, 1.0, 0.0).astype(BF16)

    def chunk(c, carry):
        rows = pl.ds(pl.multiple_of(c * L, L), L)
        lds, cums = [], []
        for b in range(B):
            ld = ld_ref[b, rows, :]
            cum = jnp.zeros_like(ld)
            rest = ld
            for _ in range(3):
                part = rest.astype(BF16)
                cum = cum + jnp.dot(tri, part, preferred_element_type=F32)
                rest = rest - part.astype(F32)
            lds.append(ld)
            cums.append(cum)
        f32 = lambda ref, b, sl: ref[b, rows, sl].astype(F32)
        ys = _run_in_lockstep([
            _rwkv_pair_chunk(f32(r_ref, b, sl), lds[b][:, sl], cums[b][:, sl], f32(k_ref, b, sl),
                             f32(v_ref, b, sl), f32(a_ref, b, sl), kk_ref[:, sl], ka_ref[:, sl],
                             rk_ref[:, sl], gw_ref[:, sl], gb_ref[:, sl], s_ref.at[n])
            for n, (b, sl) in enumerate(blocks)])
        for (b, sl), y in zip(blocks, ys):
            y_ref[b, rows, sl] = y.astype(BF16)
        return carry

    lax.fori_loop(0, r_ref.shape[1] // L, chunk, 0)


def _rwkv_recur(r, ld, k, v, a, kkp, kap, rkp, gnw, gnb, B, T, chunks_per_step=4):
    N, D = r.shape
    L = RWKV_CHUNK
    G = D // LANES
    rows = L * chunks_per_step
    seq = pl.BlockSpec((B, rows, D), lambda c: (0, c, 0))
    par = pl.BlockSpec((1, D), lambda c: (0, 0))
    b3 = lambda t: t.reshape(B, T, D)
    y = pl.pallas_call(
        functools.partial(_rwkv_recur_kernel, G, L),
        grid=(T // rows,),
        in_specs=[seq] * 5 + [par] * 5,
        out_specs=seq,
        out_shape=jax.ShapeDtypeStruct((B, T, D), BF16),
        scratch_shapes=[pltpu.VMEM((B * G, LANES, LANES), F32)],
        compiler_params=_cparams(("arbitrary",)),
        name="rwkv_recur",
    )(b3(r), b3(ld), b3(k), b3(v), b3(a), kkp, kap, rkp, gnw, gnb)
    return y.reshape(N, D)


def _swiglu_chunk(u, w1, w3, w2):
    return _dot(_silu(_dot(u, w1)) * _dot(u, w3), w2)


def _dense_ffn_kernel(h_ref, y_ref, g_ref, wo_ref, nw_ref, w1_ref, w3_ref, w2_ref, o_ref, u_ref,
                      act_ref):
    h = h_ref[...] + _dot(y_ref[...].astype(F32) * g_ref[...].astype(F32), wo_ref[...])
    u_ref[...] = _rms(h, nw_ref[...]).astype(BF16)
    for c in range(w1_ref.shape[1] // FFN_SUB):
        cs = slice(c * FFN_SUB, (c + 1) * FFN_SUB)
        u = u_ref[...]
        act_ref[:, cs] = (_silu(_dot(u, w1_ref[:, cs])) * _dot(u, w3_ref[:, cs])).astype(BF16)
    o_ref[...] = h + _dot(act_ref[...], w2_ref[...])


def _mix_out_dense_ffn(h, y, g, wo, nw, w1, w3, w2, tm=512):
    N, D = h.shape
    tm = min(tm, N)
    row = pl.BlockSpec((tm, D), lambda i: (i, 0))
    return pl.pallas_call(
        _dense_ffn_kernel,
        grid=(N // tm,),
        in_specs=[row, row, row] + [_full(w.shape) for w in (wo, nw, w1, w3, w2)],
        out_specs=row,
        out_shape=jax.ShapeDtypeStruct((N, D), F32),
        scratch_shapes=[pltpu.VMEM((tm, D), BF16), pltpu.VMEM((tm, w1.shape[1]), BF16)],
        compiler_params=_cparams(("parallel",)),
        name="dense_ffn",
    )(h, y, g, wo, nw, w1, w3, w2)


def _ret_proj_kernel(dk, h_ref, nw_ref, cs_ref, ss_ref, co_ref, so_ref, wq_ref, wk_ref, wv_ref, wg_ref,
                     q_out, k_out, v_out, g_out):
    u = _rms(h_ref[...], nw_ref[...]).astype(BF16)
    nh = wq_ref.shape[1] // dk
    cs, ss, co, so = cs_ref[0], ss_ref[0], co_ref[...], so_ref[...]
    cos = jnp.concatenate([cs * co - ss * so] * nh, axis=1)
    sin = jnp.concatenate([ss * co + cs * so] * nh, axis=1)
    lane = lax.broadcasted_iota(jnp.int32, cos.shape, 1)
    even = (lane & 1) == 0
    width = cos.shape[1]

    def rope(t):
        nxt = pltpu.roll(t, width - 1, 1)
        prv = pltpu.roll(t, 1, 1)
        return t * cos + jnp.where(even, nxt, prv) * sin

    q_out[...] = rope(_dot(u, wq_ref[...])).astype(BF16)
    k_out[...] = rope(_dot(u, wk_ref[...]) * (dk ** -0.5)).astype(BF16)
    v_out[...] = _dot(u, wv_ref[...]).astype(BF16)
    g_out[...] = _dot(u, wg_ref[...]).astype(BF16)


def _ret_proj(h, T, nw, wq, wk, wv, wg, tm=512):
    N, D = h.shape
    dk = wq.shape[1] // RET_HEADS
    tm = min(tm, T)
    tpb = T // tm
    cs, ss, co, so = _rope_tables(T, dk, tm)
    row = lambda w: pl.BlockSpec((tm, w), lambda i: (i, 0))
    start = pl.BlockSpec((1, 1, dk), lambda i: (i % tpb, 0, 0))
    return pl.pallas_call(
        functools.partial(_ret_proj_kernel, dk),
        grid=(N // tm,),
        in_specs=[row(D), _full(nw.shape), start, start, _full(co.shape), _full(so.shape)]
        + [_full(w.shape) for w in (wq, wk, wv, wg)],
        out_specs=[row(wq.shape[1]), row(wk.shape[1]), row(wv.shape[1]), row(wg.shape[1])],
        out_shape=[jax.ShapeDtypeStruct((N, w.shape[1]), BF16) for w in (wq, wk, wv, wg)],
        compiler_params=_cparams(("parallel",)),
        name="ret_proj",
    )(h, nw, cs, ss, co, so, wq, wk, wv, wg)


def _ret_head_chunk(q, k, v, g, intra, qd, kd, cd, r_ref):
    r0 = r_ref[...]
    s = _dot_nt(q, k) * intra
    inter = _dot(q, r0) * qd
    r_ref[...] = r0 * cd + _dot_tn(k.astype(F32) * kd, v)
    yield
    o = _dot(s, v) + inter
    yield
    o = o * lax.rsqrt(jnp.mean(o * o, axis=-1, keepdims=True) + RMS_EPS)
    return (_silu(g.astype(F32)) * o).astype(BF16)


def _ret_chunk_kernel(H, q_ref, k_ref, v_ref, g_ref, intra_ref, qd_ref, kd_ref, cd_ref, o_ref, r_ref):
    @pl.when(pl.program_id(1) == 0)
    def _():
        r_ref[...] = jnp.zeros_like(r_ref)

    dk = q_ref.shape[1] // H
    dv = v_ref.shape[1] // H
    ks = [slice(h * dk, (h + 1) * dk) for h in range(H)]
    vs = [slice(h * dv, (h + 1) * dv) for h in range(H)]
    outs = _run_in_lockstep([
        _ret_head_chunk(q_ref[:, ks[h]], k_ref[:, ks[h]], v_ref[:, vs[h]], g_ref[:, vs[h]],
                        intra_ref[h], qd_ref[h], kd_ref[h], cd_ref[h], r_ref.at[h])
        for h in range(H)])
    for h in range(H):
        o_ref[:, vs[h]] = outs[h]


def _ret_chunk(q, k, v, g, B, T, H):
    N = q.shape[0]
    dk = q.shape[1] // H
    dv = v.shape[1] // H
    C = min(RET_BLOCK, T)
    nc = T // C
    lg = jnp.log(1.0 - 2.0 ** (-5.0 - jnp.arange(H, dtype=F32)))
    n = jnp.arange(C, dtype=F32)
    diff = n[:, None] - n[None, :]
    intra = jnp.where(diff >= 0, jnp.exp(lg[:, None, None] * jnp.maximum(diff, 0.0)), 0.0)
    qd = jnp.broadcast_to(jnp.exp(lg[:, None] * (n[None, :] + 1.0))[:, :, None], (H, C, LANES))
    kd = jnp.broadcast_to(jnp.exp(lg[:, None] * (C - 1.0 - n[None, :]))[:, :, None], (H, C, LANES))
    cd = jnp.broadcast_to(jnp.exp(lg * C)[:, None, None], (H, 8, LANES))
    qd = jnp.concatenate([qd] * (dv // LANES), axis=2)
    kd = jnp.concatenate([kd] * (dk // LANES), axis=2)
    cd = jnp.concatenate([cd] * (dv // LANES), axis=2)[:, :1]
    seq = lambda w: pl.BlockSpec((C, w), lambda b, c: (b * nc + c, 0))
    return pl.pallas_call(
        functools.partial(_ret_chunk_kernel, H),
        grid=(B, nc),
        in_specs=[seq(H * dk), seq(H * dk), seq(H * dv), seq(H * dv)]
        + [_full(t.shape) for t in (intra, qd, kd, cd)],
        out_specs=seq(H * dv),
        out_shape=jax.ShapeDtypeStruct((N, H * dv), BF16),
        scratch_shapes=[pltpu.VMEM((H, dk, dv), F32)],
        compiler_params=_cparams(("parallel", "arbitrary")),
        name="ret_chunk",
    )(q, k, v, g, intra, qd, kd, cd)


def _ret_out_router_kernel(h_ref, o_ref, wo_ref, nw_ref, rt_ref, h_out, u_out, idx_out, gate_out):
    h = h_ref[...] + _dot(o_ref[...], wo_ref[...])
    h_out[...] = h
    u = _rms(h, nw_ref[...])
    u_out[...] = u
    logits = lax.dot_general(rt_ref[...], u, (((1,), (1,)), ((), ())),
                             precision=lax.Precision.HIGHEST, preferred_element_type=F32)
    e_iota = lax.broadcasted_iota(jnp.int32, logits.shape, 0)
    big = jnp.int32(N_EXPERTS)
    m1 = jnp.max(logits, axis=0, keepdims=True)
    i1 = jnp.min(jnp.where(logits == m1, e_iota, big), axis=0, keepdims=True)
    rest = jnp.where(e_iota == i1, -jnp.inf, logits)
    m2 = jnp.max(rest, axis=0, keepdims=True)
    i2 = jnp.min(jnp.where(rest == m2, e_iota, big), axis=0, keepdims=True)
    p2 = jnp.exp(m2 - m1)
    g1 = 1.0 / (1.0 + p2)
    idx_out[...] = jnp.concatenate([i1, i2], axis=0)
    gate_out[...] = jnp.concatenate([g1, p2 * g1], axis=0)


def _ret_out_router(h, o, wo, nw, router_t, tm=512):
    N, D = h.shape
    K = o.shape[1]
    tm = min(tm, N)
    row = lambda w: pl.BlockSpec((tm, w), lambda i: (i, 0))
    col = pl.BlockSpec((TOP_K, tm), lambda i: (0, i))
    return pl.pallas_call(
        _ret_out_router_kernel,
        grid=(N // tm,),
        in_specs=[row(D), row(K), _full(wo.shape), _full(nw.shape), _full(router_t.shape)],
        out_specs=[row(D), row(D), col, col],
        out_shape=[jax.ShapeDtypeStruct((N, D), F32), jax.ShapeDtypeStruct((N, D), F32),
                   jax.ShapeDtypeStruct((TOP_K, N), jnp.int32), jax.ShapeDtypeStruct((TOP_K, N), F32)],
        compiler_params=_cparams(("parallel",)),
        name="ret_out_router",
    )(h, o, wo, nw, router_t)


def _row_copy(src_hbm, dst_ref, sem, s, j):
    return pltpu.make_async_copy(src_hbm.at[pl.ds(s, 1), :], dst_ref.at[pl.ds(j, 1), :], sem)


def _moe_kernel(tmm, q, te_ref, src_ref, dst_ref, u_hbm, w1_ref, w3_ref, w2_ref, y_hbm,
                xbuf, xb_ref, obuf, gsem, ssem):
    i = pl.program_id(0)
    f = pl.program_id(1)
    n_i = pl.num_programs(0)
    n_f = pl.num_programs(1)
    cur = i % 2
    oth = 1 - cur

    def gather(slot, j, tok):
        return _row_copy(u_hbm, xbuf.at[slot], gsem.at[slot], tok, j)

    def scatter(slot, j, row):
        return pltpu.make_async_copy(obuf.at[slot].at[pl.ds(j, 1), :], y_hbm.at[pl.ds(row, 1), :],
                                     ssem.at[slot])

    def wait_rows(make):
        def body(j, c):
            make(j).wait()
            return c
        lax.fori_loop(0, tmm, body, 0, unroll=WAIT_UNROLL)

    @pl.when((i == 0) & (f == 0))
    def _():
        obuf[1] = jnp.zeros((tmm, obuf.shape[2]), F32)

        def first(j, c):
            gather(0, j, te_ref[n_i + j]).start()
            return c
        lax.fori_loop(0, tmm, first, 0, unroll=DMA_UNROLL)

    @pl.when(f == 0)
    def _():
        wait_rows(lambda j: gather(cur, j, 0))
        xb_ref[...] = xbuf[cur].astype(BF16)

        @pl.when(i > 0)
        def _():
            wait_rows(lambda j: scatter(cur, j, 0))
        obuf[cur] = jnp.zeros((tmm, obuf.shape[2]), F32)

    def issue_rows(slot, first):
        for j in range(first, first + q):
            gather(slot, j, src_ref[j]).start()
            scatter(slot, j, dst_ref[j]).start(priority=j % 2)

    for slot in range(2):
        for ff in range(tmm // q):
            pl.when((oth == slot) & (f == ff))(functools.partial(issue_rows, slot, ff * q))

    @pl.when(i < te_ref[n_i + 2 * tmm])
    def _():
        for c in range(w1_ref.shape[2] // MOE_SUB):
            cs = slice(c * MOE_SUB, (c + 1) * MOE_SUB)
            obuf[cur] += _swiglu_chunk(xb_ref[...], w1_ref[0, :, cs], w3_ref[0, :, cs], w2_ref[0, cs, :])

    @pl.when((f == n_f - 1) & (i == n_i - 1))
    def _():
        def last(j, c):
            scatter(cur, j, te_ref[n_i + tmm + j]).start()
            return c
        lax.fori_loop(0, tmm, last, 0, unroll=DMA_UNROLL)
        wait_rows(lambda j: scatter(oth, j, 0))
        wait_rows(lambda j: scatter(cur, j, 0))
        wait_rows(lambda j: gather(oth, j, 0))


def _moe_experts(u, plan, src_next, dst_prev, w1, w3, w2, n_rows_out, tf=1792):
    N, D = u.shape
    Fd = w1.shape[2]
    tmm = MOE_TILE
    n_tiles = src_next.shape[0] // tmm
    nf = Fd // tf
    q = tmm // nf
    smem = lambda: pl.BlockSpec((tmm,), lambda i, f, te: (i,), memory_space=pltpu.SMEM)
    gs = pltpu.PrefetchScalarGridSpec(
        num_scalar_prefetch=1,
        grid=(n_tiles, nf),
        in_specs=[smem(), smem(), pl.BlockSpec(memory_space=pl.ANY),
                  pl.BlockSpec((1, D, tf), lambda i, f, te: (te[i], 0, f)),
                  pl.BlockSpec((1, D, tf), lambda i, f, te: (te[i], 0, f)),
                  pl.BlockSpec((1, tf, D), lambda i, f, te: (te[i], f, 0))],
        out_specs=pl.BlockSpec(memory_space=pl.ANY),
        scratch_shapes=[pltpu.VMEM((2, tmm, D), F32), pltpu.VMEM((tmm, D), BF16),
                        pltpu.VMEM((2, tmm, D), F32), pltpu.SemaphoreType.DMA((2,)),
                        pltpu.SemaphoreType.DMA((2,))],
    )
    return pl.pallas_call(
        functools.partial(_moe_kernel, tmm, q),
        grid_spec=gs,
        out_shape=jax.ShapeDtypeStruct((n_rows_out, D), F32),
        compiler_params=_cparams(("arbitrary", "arbitrary")),
        name="moe_experts",
    )(plan, src_next, dst_prev, u, w1, w3, w2)


def _combine_kernel(h_ref, y0_ref, y1_ref, gate_ref, nw_ref, o_ref):
    gate = gate_ref[...]
    moe = gate[:, 0:1] * y0_ref[...] + gate[:, 1:2] * y1_ref[...]
    o_ref[...] = _rms(h_ref[...] + moe, nw_ref[...])


def _combine(h, y, gate, nw, tc=512):
    N, D = h.shape
    tc = min(tc, N)
    nb = N // tc
    return pl.pallas_call(
        _combine_kernel,
        grid=(nb,),
        in_specs=[pl.BlockSpec((tc, D), lambda i: (i, 0)),
                  pl.BlockSpec((tc, D), lambda i: (i, 0)),
                  pl.BlockSpec((tc, D), lambda i: (nb + i, 0)),
                  pl.BlockSpec((tc, TOP_K), lambda i: (i, 0)),
                  _full(nw.shape)],
        out_specs=pl.BlockSpec((tc, D), lambda i: (i, 0)),
        out_shape=jax.ShapeDtypeStruct((N, D), F32),
        compiler_params=_cparams(("parallel",)),
        name="moe_combine",
    )(h, y, y, gate, nw)


def _route_plan(idx, N):
    tmm = MOE_TILE
    n_asg = TOP_K * N
    n_tiles = n_asg // tmm + N_EXPERTS
    R = n_tiles * tmm
    flat_e = idx.reshape(-1)
    asg = jnp.arange(n_asg, dtype=jnp.int32)
    order = jnp.sort(flat_e * n_asg + asg) % n_asg
    counts = jnp.sum((flat_e[None, :] == jnp.arange(N_EXPERTS, dtype=jnp.int32)[:, None]).astype(jnp.int32),
                     axis=1)
    ptiles = (counts + tmm - 1) // tmm
    tile_end = jnp.cumsum(ptiles)
    tile_start = tile_end - ptiles
    cum_incl = jnp.cumsum(counts)
    cum_excl = cum_incl - counts
    t = jnp.arange(n_tiles, dtype=jnp.int32)
    tile_e = jnp.sum((t[:, None] >= tile_end[None, :]).astype(jnp.int32), axis=1)
    tile_expert = jnp.minimum(tile_e, N_EXPERTS - 1)
    span = n_asg + tmm
    order_pad = jnp.concatenate([order, jnp.zeros((span,), jnp.int32)])
    rows = jnp.zeros((R + span,), jnp.int32)
    for e in range(N_EXPERTS):
        run = lax.dynamic_slice(order_pad, (cum_excl[e],), (span,))
        rows = lax.dynamic_update_slice(rows, run, (tile_start[e] * tmm,))
    k = t - tile_start[tile_expert]
    n_real = jnp.where(tile_e < N_EXPERTS, jnp.clip(counts[tile_expert] - k * tmm, 0, tmm), 0)
    j = jnp.arange(tmm, dtype=jnp.int32)
    real_before = jnp.concatenate([cum_incl, jnp.full((1,), n_asg, jnp.int32)])[tile_e]
    spare = n_asg + (t * tmm - real_before)[:, None] + j[None, :]
    dst = jnp.where(j[None, :] < n_real[:, None], rows[:R].reshape(n_tiles, tmm), spare).reshape(R)
    src = jnp.where(dst < n_asg, dst % N, 0)
    zeros = jnp.zeros((tmm,), jnp.int32)
    before_first = R + jnp.arange(tmm, dtype=jnp.int32)
    src_next = jnp.concatenate([src[tmm:], zeros])
    dst_prev = jnp.concatenate([before_first, dst[:R - tmm]])
    plan = jnp.concatenate([tile_expert, src[:tmm], dst[R - tmm:], tile_end[-1:]]).astype(jnp.int32)
    return plan, src_next, dst_prev, R + tmm


def _rope_tables(T, dk, tm):
    angle = 1.0 / (ROPE_BASE ** jnp.linspace(0.0, 1.0, dk // 2, dtype=F32))
    angle = jnp.repeat(angle, 2)
    sign = jnp.where(jnp.arange(dk) % 2 == 0, -1.0, 1.0).astype(F32)
    start = (jnp.arange(T // tm, dtype=F32) * tm)[:, None] * angle[None, :]
    off = jnp.arange(tm, dtype=F32)[:, None] * angle[None, :]
    return (jnp.cos(start)[:, None, :], (jnp.sin(start) * sign[None, :])[:, None, :],
            jnp.cos(off), jnp.sin(off) * sign[None, :])


def kernel(x, norm_mix, norm_ffn, norm_final, a_mu, a_wr, a_wk, a_wv, a_wo, a_w0, a_w1, a_w2, a_a0, a_a1, a_a2, a_g1, a_g2, a_kk, a_ka, a_rk, a_gn_w, a_gn_b, b_wq, b_wk, b_wv, b_wg, b_wo, f_w1, f_w3, f_w2, m_router, m_w1, m_w3, m_w2):
    B, T, D = x.shape
    N = B * T
    bf = lambda w: w.astype(BF16)
    vec = lambda w: w.reshape(1, -1).astype(F32)
    h = x.reshape(N, D)

    r, ld, k, v, a, g = _rwkv_proj(
        h, T, vec(norm_mix[0]), a_mu[0], bf(a_wr[0]), bf(a_wk[0]), bf(a_wv[0]), bf(a_w1[0]), bf(a_w2[0]),
        bf(a_a1[0]), bf(a_a2[0]), bf(a_g1[0]), bf(a_g2[0]), vec(a_w0[0]), vec(a_a0[0]))
    y = _rwkv_recur(r, ld, k, v, a, vec(a_kk[0]), vec(a_ka[0]), vec(a_rk[0]), vec(a_gn_w[0]),
                    vec(a_gn_b[0]), B, T)
    h = _mix_out_dense_ffn(h, y, g, bf(a_wo[0]), vec(norm_ffn[0]), bf(f_w1[0]), bf(f_w3[0]), bf(f_w2[0]))

    q, kr, vr, gr = _ret_proj(h, T, vec(norm_mix[1]), bf(b_wq[0]), bf(b_wk[0]), bf(b_wv[0]), bf(b_wg[0]))
    o = _ret_chunk(q, kr, vr, gr, B, T, RET_HEADS)
    h, u, idx, gate = _ret_out_router(h, o, bf(b_wo[0]), vec(norm_ffn[1]), m_router[0].T.astype(F32))
    plan, src_next, dst_prev, n_rows = _route_plan(idx, N)
    y = _moe_experts(u, plan, src_next, dst_prev, bf(m_w1[0]), bf(m_w3[0]), bf(m_w2[0]), n_rows)
    out = _combine(h, y, gate.T, vec(norm_final))
    return out.reshape(B, T, D)
```
